```python
import math
import jax
import jax.numpy as jnp
from jax import lax
import numpy as np

D_MODEL = 2048
BATCH = 1
SEQ = 8192
DEPTH = 2

CTX_LEN = 256
GRID_W = 64

D_INNER = 2 * D_MODEL
SSM_HEAD_DIM = 64
SSM_HEADS = D_INNER // SSM_HEAD_DIM
SSM_GROUPS = 8
HEADS_PER_GROUP = SSM_HEADS // SSM_GROUPS
SSM_STATE = 128
SSM_CONV = 5
CHUNK = 128
D_BC = SSM_GROUPS * SSM_STATE
D_XBC = D_INNER + 2 * D_BC

D_CONV = D_MODEL
CONV_K = 31

N_EXPERTS = 16
EC_CAPACITY_FACTOR = 2
D_EXPERT = 3 * D_MODEL // 2

N_MOD = 6
ALPHA = (2 * DEPTH) ** 0.25
BETA = (8 * DEPTH) ** -0.25
LN_EPS = 1e-5

O_Z = 0
O_XBC = O_Z + D_INNER
O_DT = O_XBC + D_XBC
O_GLU = O_DT + 2 * SSM_HEADS
O_GATE = O_GLU + 2 * D_CONV
D_PROJ = O_GATE + 2 * D_MODEL

kernel_name = "hybrid_ssd_conformer_ecmoe_prefix_dit"


def layer_norm(x, g, b):
    xf = x.astype(jnp.float32)
    mu = jnp.mean(xf, axis=-1, keepdims=True)
    xc = xf - mu
    var = jnp.mean(xc * xc, axis=-1, keepdims=True)
    return (xc * lax.rsqrt(var + LN_EPS) * g + b).astype(x.dtype)


def modulate(x, shift, scale):
    return x * (1 + scale) + shift


def dwconv(x, w, b):
    k = w.shape[0]
    y = lax.conv_general_dilated(
        x, w[:, None, :].astype(x.dtype), window_strides=(1,),
        padding=[(k // 2, k // 2)], dimension_numbers=('NWC', 'WIO', 'NWC'),
        feature_group_count=x.shape[-1])
    return y + b


def axial_dwconv(h, w, b):
    bn, n, ch = h.shape
    rows = n // GRID_W
    half = ch // 2
    hh = h[..., :half].reshape(bn * rows, GRID_W, half)
    yh = dwconv(hh, w[:, :half], b[:half]).reshape(bn, n, half)
    hv = h[..., half:].reshape(bn, rows, GRID_W, ch - half).transpose(0, 2, 1, 3)
    hv = hv.reshape(bn * GRID_W, rows, ch - half)
    yv = dwconv(hv, w[:, half:], b[half:]).reshape(bn, GRID_W, rows, ch - half)
    yv = yv.transpose(0, 2, 1, 3).reshape(bn, n, ch - half)
    return jnp.concatenate([yh, yv], axis=-1)


def ssd_inputs(xbc, dt_raw, dt_bias):
    bn, L, _ = xbc.shape
    xs = xbc[..., :D_INNER].reshape(bn, L, SSM_GROUPS, HEADS_PER_GROUP, SSM_HEAD_DIM)
    bm = xbc[..., D_INNER:D_INNER + D_BC].reshape(bn, L, SSM_GROUPS, SSM_STATE)
    cm = xbc[..., D_INNER + D_BC:].reshape(bn, L, SSM_GROUPS, SSM_STATE)
    dt = jax.nn.softplus((dt_raw + dt_bias.reshape(-1)).astype(jnp.float32))
    dt = dt.reshape(bn, L, 2, SSM_GROUPS, HEADS_PER_GROUP)
    return xs, bm, cm, dt


def ssd_chunked(xdt, a, bm, cm, h0):
    bn, L, g, e, p = xdt.shape
    nc = L // CHUNK
    xdt = xdt.reshape(bn, nc, CHUNK, g, e, p)
    a = a.reshape(bn, nc, CHUNK, g, e)
    bm = bm.reshape(bn, nc, CHUNK, g, -1)
    cm = cm.reshape(bn, nc, CHUNK, g, -1)
    a_cs = jnp.cumsum(a, axis=2)
    lower = jnp.tril(jnp.ones((CHUNK, CHUNK), dtype=bool))
    seg = a_cs[:, :, :, None] - a_cs[:, :, None, :]
    decay = jnp.exp(jnp.where(lower[:, :, None, None], seg, -jnp.inf))
    scores = jnp.einsum('bcign,bcjgn->bcijg', cm, bm)
    y_diag = jnp.einsum('bcijg,bcijge,bcjgep->bcigep', scores, decay, xdt)
    decay_to_end = jnp.exp(a_cs[:, :, -1:] - a_cs)
    chunk_states = jnp.einsum('bcjgn,bcjge,bcjgep->bcgepn', bm, decay_to_end, xdt)
    chunk_decay = jnp.exp(a_cs[:, :, -1])

    def step(h, inp):
        dec, s = inp
        return h * dec[..., None, None] + s, h

    h_final, h_enter = lax.scan(step, h0, (jnp.moveaxis(chunk_decay, 1, 0),
                                           jnp.moveaxis(chunk_states, 1, 0)))
    h_enter = jnp.moveaxis(h_enter, 0, 1)
    y_off = jnp.einsum('bcign,bcgepn,bcige->bcigep', cm, h_enter, jnp.exp(a_cs))
    return (y_diag + y_off).reshape(bn, L, g, e, p), h_final


def bidirectional_ssd(ssm_c, ssm_l, a_log):
    xs_c, bm_c, cm_c, dt_c = ssm_c
    xs_l, bm_l, cm_l, dt_l = ssm_l
    bn = xs_l.shape[0]
    y_c = jnp.zeros(xs_c.shape, jnp.float32)
    y_l = jnp.zeros(xs_l.shape, jnp.float32)
    for d in range(2):
        a_rate = -jnp.exp(a_log[d].astype(jnp.float32)).reshape(SSM_GROUPS, HEADS_PER_GROUP)

        def direction_inputs(xs, bm, cm, dt):
            dtd = dt[:, :, d]
            args = (xs.astype(jnp.float32) * dtd[..., None], dtd * a_rate,
                    bm.astype(jnp.float32), cm.astype(jnp.float32))
            if d == 1:
                args = tuple(jnp.flip(t, axis=1) for t in args)
            return args

        h0 = jnp.zeros((bn, SSM_GROUPS, HEADS_PER_GROUP, SSM_HEAD_DIM, SSM_STATE), jnp.float32)
        yc_d, h_ctx = ssd_chunked(*direction_inputs(xs_c, bm_c, cm_c, dt_c), h0)
        yl_d, _ = ssd_chunked(*direction_inputs(xs_l, bm_l, cm_l, dt_l), h_ctx)
        if d == 1:
            yc_d = jnp.flip(yc_d, axis=1)
            yl_d = jnp.flip(yl_d, axis=1)
        y_c = y_c + yc_d
        y_l = y_l + yl_d
    return y_c, y_l


def gated_group_rmsnorm(y, z, w):
    h = (y * jax.nn.silu(z)).astype(jnp.float32)
    hg = h.reshape(*h.shape[:-1], SSM_GROUPS, -1)
    hg = hg * lax.rsqrt(jnp.mean(hg * hg, axis=-1, keepdims=True) + LN_EPS)
    return (hg.reshape(h.shape) * w).astype(y.dtype)


def mixer_output(p, y, xs, conv_fn, d_skip, norm_w, w_ssm_out, ln_g, ln_b, w_conv_out, w_o):
    bn, L, _ = p.shape
    z = p[..., O_Z:O_XBC]
    glu = p[..., O_GLU:O_GATE]
    gate = p[..., O_GATE:]
    y = y + d_skip.reshape(SSM_GROUPS, HEADS_PER_GROUP)[..., None] * xs
    y = y.reshape(bn, L, D_INNER).astype(p.dtype)
    y_ssm = gated_group_rmsnorm(y, z, norm_w) @ w_ssm_out
    ga, gb = jnp.split(glu, 2, axis=-1)
    hcv = conv_fn(ga * jax.nn.sigmoid(gb))
    hcv = jax.nn.silu(layer_norm(hcv, ln_g, ln_b))
    y_conv = hcv @ w_conv_out
    g_ssm, g_conv = jnp.split(jax.nn.sigmoid(gate), 2, axis=-1)
    return (g_ssm * y_ssm + g_conv * y_conv) @ w_o


def expert_choice(u, w_router, w_gate, w_up, w_down):
    bn, n, dm = u.shape
    cap = max(1, EC_CAPACITY_FACTOR * n // N_EXPERTS)
    aff = jax.nn.softmax((u @ w_router).astype(jnp.float32), axis=-1)
    g, idx = lax.top_k(jnp.swapaxes(aff, 1, 2), cap)
    xe = jax.vmap(lambda ub, ib: ub[ib])(u, idx)
    h = jax.nn.silu(jnp.einsum('becd,edf->becf', xe, w_gate)) * jnp.einsum('becd,edf->becf', xe, w_up)
    ye = jnp.einsum('becf,efd->becd', h, w_down) * g[..., None].astype(u.dtype)
    return jax.vmap(lambda ib, yb: jnp.zeros((n, dm), yb.dtype).at[ib.reshape(-1)].add(yb.reshape(-1, dm)))(idx, ye)


def setup_inputs(seed: int = 0) -> dict:
    key = jax.random.key(seed)
    ks = jax.random.split(key, 32)
    f32 = jnp.float32
    L = DEPTH

    def nrm(k, shape, scale):
        return jax.random.normal(k, shape, f32) * scale

    dt0 = jnp.exp(jax.random.uniform(ks[8], (L, 2, SSM_HEADS), f32, math.log(1e-3), math.log(1e-1)))
    dt_bias = dt0 + jnp.log(-jnp.expm1(-dt0))
    a_log = jnp.log(jax.random.uniform(ks[9], (L, 2, SSM_HEADS), f32, 1.0, 16.0))
    return {
        'x': nrm(ks[0], (BATCH, SEQ, D_MODEL), 1.0),
        'c': nrm(ks[1], (BATCH, D_MODEL), 1.0),
        'ctx': nrm(ks[2], (BATCH, CTX_LEN, D_MODEL), 1.0),
        'c_ctx': nrm(ks[3], (D_MODEL,), 1.0),
        'w_ada': nrm(ks[4], (L, D_MODEL, N_MOD * D_MODEL), 0.5 * D_MODEL ** -0.5),
        'b_ada': nrm(ks[5], (L, N_MOD * D_MODEL), 0.01),
        'w_in': nrm(ks[6], (L, D_MODEL, D_PROJ), D_MODEL ** -0.5),
        'ssm_conv_w': nrm(ks[7], (L, SSM_CONV, D_XBC), SSM_CONV ** -0.5),
        'ssm_conv_b': nrm(ks[10], (L, D_XBC), 0.02),
        'ssm_dt_bias': dt_bias,
        'ssm_a_log': a_log,
        'ssm_d': 1.0 + nrm(ks[11], (L, SSM_HEADS), 0.02),
        'ssm_norm_w': 1.0 + nrm(ks[12], (L, D_INNER), 0.02),
        'w_ssm_out': nrm(ks[13], (L, D_INNER, D_MODEL), D_INNER ** -0.5),
        'conv_dw_w': nrm(ks[14], (L, CONV_K, D_CONV), CONV_K ** -0.5),
        'conv_dw_b': nrm(ks[15], (L, D_CONV), 0.02),
        'conv_ln_g': 1.0 + nrm(ks[16], (L, D_CONV), 0.02),
        'conv_ln_b': nrm(ks[17], (L, D_CONV), 0.02),
        'w_conv_out': nrm(ks[18], (L, D_CONV, D_MODEL), D_CONV ** -0.5),
        'w_o': nrm(ks[19], (L, D_MODEL, D_MODEL), BETA * D_MODEL ** -0.5),
        'ln1_g': 1.0 + nrm(ks[20], (L, D_MODEL), 0.02),
        'ln1_b': nrm(ks[21], (L, D_MODEL), 0.02),
        'w_router': nrm(ks[22], (L, D_MODEL, N_EXPERTS), D_MODEL ** -0.5),
        'w_exp_gate': nrm(ks[23], (L, N_EXPERTS, D_MODEL, D_EXPERT), D_MODEL ** -0.5),
        'w_exp_up': nrm(ks[24], (L, N_EXPERTS, D_MODEL, D_EXPERT), D_MODEL ** -0.5),
        'w_exp_down': nrm(ks[25], (L, N_EXPERTS, D_EXPERT, D_MODEL), BETA * D_EXPERT ** -0.5),
        'ln2_g': 1.0 + nrm(ks[26], (L, D_MODEL), 0.02),
        'ln2_b': nrm(ks[27], (L, D_MODEL), 0.02),
    }


def reference(x, c, ctx, c_ctx, w_ada, b_ada, w_in, ssm_conv_w, ssm_conv_b, ssm_dt_bias,
              ssm_a_log, ssm_d, ssm_norm_w, w_ssm_out, conv_dw_w, conv_dw_b, conv_ln_g,
              conv_ln_b, w_conv_out, w_o, ln1_g, ln1_b, w_router, w_exp_gate, w_exp_up,
              w_exp_down, ln2_g, ln2_b):
    xl, xc = x, ctx
    for i in range(DEPTH):
        last = i == DEPTH - 1
        sh1_l, sc1_l, g1_l, sh2_l, sc2_l, g2_l = jnp.split(
            (jax.nn.silu(c) @ w_ada[i] + b_ada[i])[:, None, :], N_MOD, axis=-1)
        sh1_c, sc1_c, g1_c, sh2_c, sc2_c, g2_c = jnp.split(
            jax.nn.silu(c_ctx) @ w_ada[i] + b_ada[i], N_MOD, axis=-1)

        u_l = modulate(xl, sh1_l, sc1_l)
        u_c = modulate(xc, sh1_c, sc1_c)
        p_l = u_l @ w_in[i]
        if last:
            p_c_scan = u_c @ w_in[i][:, O_XBC:O_GLU]
        else:
            p_c = u_c @ w_in[i]
            p_c_scan = p_c[..., O_XBC:O_GLU]
        p_l_scan = p_l[..., O_XBC:O_GLU]

        def scan_inputs(ps):
            xbc = jax.nn.silu(dwconv(ps[..., :D_XBC], ssm_conv_w[i], ssm_conv_b[i]))
            return ssd_inputs(xbc, ps[..., D_XBC:], ssm_dt_bias[i])

        ssm_c = scan_inputs(p_c_scan)
        ssm_l = scan_inputs(p_l_scan)
        y_c, y_l = bidirectional_ssd(ssm_c, ssm_l, ssm_a_log[i])

        out_l = mixer_output(p_l, y_l, ssm_l[0],
                             lambda h: axial_dwconv(h, conv_dw_w[i], conv_dw_b[i]),
                             ssm_d[i], ssm_norm_w[i], w_ssm_out[i], conv_ln_g[i], conv_ln_b[i],
                             w_conv_out[i], w_o[i])
        xl = layer_norm(ALPHA * xl + g1_l * out_l, ln1_g[i], ln1_b[i])

        moe_l = expert_choice(modulate(xl, sh2_l, sc2_l), w_router[i], w_exp_gate[i],
                              w_exp_up[i], w_exp_down[i])
        xl = layer_norm(ALPHA * xl + g2_l * moe_l, ln2_g[i], ln2_b[i])

        if not last:
            out_c = mixer_output(p_c, y_c, ssm_c[0],
                                 lambda h: dwconv(h, conv_dw_w[i], conv_dw_b[i]),
                                 ssm_d[i], ssm_norm_w[i], w_ssm_out[i], conv_ln_g[i], conv_ln_b[i],
                                 w_conv_out[i], w_o[i])
            xc = layer_norm(ALPHA * xc + g1_c * out_c, ln1_g[i], ln1_b[i])
            moe_c = expert_choice(modulate(xc, sh2_c, sc2_c), w_router[i], w_exp_gate[i],
                                  w_exp_up[i], w_exp_down[i])
            xc = layer_norm(ALPHA * xc + g2_c * moe_c, ln2_g[i], ln2_b[i])
    return xl
```

```python
import functools

import jax
import jax.numpy as jnp
from jax import lax
from jax.experimental import pallas as pl
from jax.experimental.pallas import tpu as pltpu

F32 = jnp.float32
BF16 = jnp.bfloat16

HEAD_DIM = 64
GROUPS = 8
STATE = 128
SSM_CONV = 5
CHUNK = 128
CONV_K = 31
GRID_W = 64
N_EXPERTS = 16
EC_CAPACITY_FACTOR = 2
N_MOD = 6
LN_EPS = 1e-5

LANES = 128
SUBLANES = 8
VMEM_LIMIT = 56 * 1024 * 1024

ROW_TILE = 256
HALO = SUBLANES


def _cparams(*sem):
    return pltpu.CompilerParams(dimension_semantics=sem, vmem_limit_bytes=VMEM_LIMIT)


def _sigmoid(v):
    return jax.nn.sigmoid(v)


def _silu(v):
    return v * jax.nn.sigmoid(v)


def _softplus(v):
    return jnp.maximum(v, 0.0) + jnp.log1p(jnp.exp(-jnp.abs(v)))


def _ada_kernel(c_ref, w_ref, b_ref, o_ref):
    a = _silu(c_ref[...])
    o_ref[...] = jnp.dot(a, w_ref[...], preferred_element_type=F32,
                         precision=lax.Precision.HIGHEST) + b_ref[...]


def _ada_call(cvec, w_ada, b_ada):
    depth, d, nm = w_ada.shape
    tn = 1024 if nm % 1024 == 0 else nm
    return pl.pallas_call(
        _ada_kernel,
        grid=(depth, nm // tn),
        in_specs=[
            pl.BlockSpec((SUBLANES, d), lambda l, j: (0, 0)),
            pl.BlockSpec((None, d, tn), lambda l, j: (l, 0, j)),
            pl.BlockSpec((None, 1, tn), lambda l, j: (l, 0, j)),
        ],
        out_specs=pl.BlockSpec((None, SUBLANES, tn), lambda l, j: (l, 0, j)),
        out_shape=jax.ShapeDtypeStruct((depth, SUBLANES, nm), F32),
        compiler_params=_cparams("arbitrary", "arbitrary"),
        name="adaln",
    )(cvec, w_ada, b_ada.reshape(depth, 1, nm))


def _mm_kernel(*refs, n_pro, n_epi, pro_fn, epi_fn):
    x_ref = refs[0]
    pro = refs[1:1 + n_pro]
    w_ref = refs[1 + n_pro]
    epi = refs[2 + n_pro:2 + n_pro + n_epi]
    o_ref = refs[2 + n_pro + n_epi]
    wb_ref = refs[3 + n_pro + n_epi]

    @pl.when(pl.program_id(1) == 0)
    def _():
        wb_ref[...] = w_ref[...].astype(BF16)

    x = x_ref[...]
    if pro_fn is not None:
        x = pro_fn(x, *[r[...] for r in pro])
    acc = jnp.dot(x.astype(BF16), wb_ref[...], preferred_element_type=F32)
    if epi_fn is not None:
        acc = epi_fn(acc, *[r[...] for r in epi])
    o_ref[...] = acc.astype(o_ref.dtype)


def _mm_call(x, w, layer, col_off, n_out, *, rows, tn, name, out_dtype=F32,
             pro_fn=None, pro_specs=(), pro_args=(), epi_fn=None, epi_specs=(), epi_args=()):
    k = x.shape[1]
    tm = ROW_TILE
    assert rows % tm == 0 and n_out % tn == 0 and col_off % tn == 0
    joff = col_off // tn
    kern = functools.partial(_mm_kernel, n_pro=len(pro_args), n_epi=len(epi_args),
                             pro_fn=pro_fn, epi_fn=epi_fn)
    return pl.pallas_call(
        kern,
        grid=(n_out // tn, rows // tm),
        in_specs=[pl.BlockSpec((tm, k), lambda j, i: (i, 0)), *pro_specs,
                  pl.BlockSpec((None, k, tn), lambda j, i: (layer, 0, j + joff)), *epi_specs],
        out_specs=pl.BlockSpec((tm, tn), lambda j, i: (i, j)),
        out_shape=jax.ShapeDtypeStruct((rows, n_out), out_dtype),
        scratch_shapes=[pltpu.VMEM((k, tn), BF16)],
        compiler_params=_cparams("arbitrary", "arbitrary"),
        name=name,
    )(x, *pro_args, w, *epi_args)


def _mod_spec(n_lat_tiles):
    def imap(j, i):
        return (jnp.where(i >= n_lat_tiles, 1, 0), 0, 0)
    return imap


def _modulate_pro(x, mod):
    return x * (1.0 + mod[1:2, :]) + mod[0:1, :]


def _ssd_chunk_of_step(s, direction, nl, ncc):
    if direction == 0:
        return jnp.where(s < ncc, nl + s, s - ncc)
    return jnp.where(s < ncc, nl + ncc - 1 - s, nl - 1 - (s - ncc))


def _ssd_kernel(*refs, direction, nl, ncc, d_inner, has_prev):
    if has_prev:
        (xm_ref, xp_ref, xn_ref, dt_ref, cw_ref, cb_ref, dtb_ref, arow_ref, e_ref, dsk_ref,
         yprev_ref, y_ref, state_ref, ext_ref, xc_ref) = refs
    else:
        (xm_ref, xp_ref, xn_ref, dt_ref, cw_ref, cb_ref, dtb_ref, arow_ref, e_ref, dsk_ref,
         y_ref, state_ref, ext_ref, xc_ref) = refs
        yprev_ref = None
    q = CHUNK
    d_bc = GROUPS * STATE
    d_xbc = d_inner + 2 * d_bc
    gw = d_inner // GROUPS
    hpg = gw // HEAD_DIM
    n_heads = d_inner // HEAD_DIM

    s = pl.program_id(0)
    c = _ssd_chunk_of_step(s, direction, nl, ncc)

    @pl.when(s == 0)
    def _():
        state_ref[...] = jnp.zeros_like(state_ref)

    seg_first = jnp.logical_or(c == 0, c == nl)
    seg_last = jnp.logical_or(c == nl - 1, c == nl + ncc - 1)
    ext_ref[0:HALO, :] = jnp.where(seg_first, 0.0, xp_ref[...])
    ext_ref[HALO:HALO + q, :] = xm_ref[...]
    ext_ref[HALO + q:HALO + q + HALO, :] = jnp.where(seg_last, 0.0, xn_ref[...])
    ct = 512
    base = HALO - SSM_CONV // 2
    for j in range(d_xbc // ct):
        cs = slice(j * ct, (j + 1) * ct)
        acc = jnp.broadcast_to(cb_ref[:, cs], (q, ct))
        for k in range(SSM_CONV):
            acc = acc + cw_ref[k:k + 1, cs] * ext_ref[base + k:base + k + q, cs]
        xc_ref[:, cs] = _silu(acc)

    dt = _softplus(dt_ref[...] + dtb_ref[...])
    a = dt * arow_ref[...]
    ri = lax.broadcasted_iota(jnp.int32, (q, q), 0)
    ci = lax.broadcasted_iota(jnp.int32, (q, q), 1)
    causal = (ri >= ci) if direction == 0 else (ri <= ci)
    cum = jnp.dot(causal.astype(F32), a, preferred_element_type=F32,
                  precision=lax.Precision.HIGHEST)
    tot = cum[q - 1:q, :] if direction == 0 else cum[0:1, :]
    e_in = jnp.exp(cum)
    w_st = dt * jnp.exp(tot - cum)
    cum_t = cum.T
    dt_t = dt.T

    stack = jnp.concatenate([w_st, e_in], axis=0)
    hi = stack.astype(BF16)
    lo = (stack - hi.astype(F32)).astype(BF16)
    spread = jnp.dot(jnp.concatenate([hi, lo], axis=1), e_ref[...],
                     preferred_element_type=F32)
    ext_ref[0:2 * q, 0:d_inner] = spread
    dec_row = q - 1 if direction == 0 else 0

    lane = lax.broadcasted_iota(jnp.int32, (q, LANES), 1)
    first_head = lane < HEAD_DIM
    hoff = direction * n_heads

    for g in range(GROUPS):
        bg = xc_ref[:, d_inner + g * STATE:d_inner + (g + 1) * STATE]
        cg = xc_ref[:, d_inner + d_bc + g * STATE:d_inner + d_bc + (g + 1) * STATE]
        bgb = bg.astype(BF16)
        cgb = cg.astype(BF16)
        scores = lax.dot_general(cgb, bgb, (((1,), (1,)), ((), ())),
                                 preferred_element_type=F32)
        gs = slice(g * gw, (g + 1) * gw)
        st = state_ref[:, gs]
        y_g = jnp.dot(cgb, st.astype(BF16), preferred_element_type=F32) * ext_ref[q:2 * q, gs]
        xg = xc_ref[:, gs]
        xw = (xg * ext_ref[0:q, gs]).astype(BF16)
        s_new = jnp.dot(bg.T.astype(BF16), xw, preferred_element_type=F32)
        state_ref[:, gs] = st * ext_ref[q + dec_row:q + dec_row + 1, gs] + s_new
        parts = []
        for p in range(hpg // 2):
            slab = xg[:, p * LANES:(p + 1) * LANES]
            y_p = None
            for half in range(2):
                h = hoff + g * hpg + 2 * p + half
                seg = cum[:, h:h + 1] - cum_t[h:h + 1, :]
                m = scores * jnp.exp(jnp.where(causal, seg, -jnp.inf)) * dt_t[h:h + 1, :]
                keep = first_head if half == 0 else jnp.logical_not(first_head)
                xh = jnp.where(keep, slab, 0.0).astype(BF16)
                t = jnp.dot(m.astype(BF16), xh, preferred_element_type=F32)
                y_p = t if y_p is None else y_p + t
            parts.append(y_p)
        y_g = y_g + jnp.concatenate(parts, axis=1)
        if direction == 0:
            y_g = y_g + dsk_ref[:, gs] * xg
        if yprev_ref is not None:
            y_g = y_g + yprev_ref[:, gs]
        y_ref[:, gs] = y_g


def _ssd_call(xbc, dtraw, conv_w, conv_b, dt_bias, a_row, e_mat, d_row, yprev, *,
              direction, nl, ncc, d_inner):
    rows = (nl + ncc) * CHUNK
    d_xbc = d_inner + 2 * GROUPS * STATE
    nh2 = dtraw.shape[1]
    q = CHUNK
    qh = q // HALO
    cmap = functools.partial(_ssd_chunk_of_step, direction=direction, nl=nl, ncc=ncc)
    nchunks = nl + ncc
    xbc_cols = pl.BlockSpec((q, d_xbc), lambda s: (cmap(s), 0))
    prev_spec = pl.BlockSpec((HALO, d_xbc), lambda s: (jnp.maximum(cmap(s) * qh - 1, 0), 0))
    next_spec = pl.BlockSpec((HALO, d_xbc),
                             lambda s: (jnp.minimum((cmap(s) + 1) * qh, nchunks * qh - 1), 0))
    full = lambda shape: pl.BlockSpec(shape, lambda s: (0,) * len(shape))
    in_specs = [xbc_cols, prev_spec, next_spec,
                pl.BlockSpec((q, nh2), lambda s: (cmap(s), 0)),
                full((SUBLANES, d_xbc)), full((1, d_xbc)), full((1, nh2)), full((1, nh2)),
                full(e_mat.shape), full((1, d_inner))]
    args = [xbc, xbc, xbc, dtraw, conv_w, conv_b, dt_bias, a_row, e_mat, d_row]
    if yprev is not None:
        in_specs.append(pl.BlockSpec((q, d_inner), lambda s: (cmap(s), 0)))
        args.append(yprev)
    kern = functools.partial(_ssd_kernel, direction=direction, nl=nl, ncc=ncc,
                             d_inner=d_inner, has_prev=yprev is not None)
    ext_rows = max(q + 2 * HALO, 2 * q)
    return pl.pallas_call(
        kern,
        grid=(nchunks,),
        in_specs=in_specs,
        out_specs=pl.BlockSpec((q, d_inner), lambda s: (cmap(s), 0)),
        out_shape=jax.ShapeDtypeStruct((rows, d_inner), F32),
        scratch_shapes=[pltpu.VMEM((STATE, d_inner), F32),
                        pltpu.VMEM((ext_rows, d_xbc), F32),
                        pltpu.VMEM((q, d_xbc), F32)],
        compiler_params=_cparams("arbitrary"),
        name="ssd_fwd" if direction == 0 else "ssd_bwd",
    )(*args)


def _gnorm_kernel(y_ref, z_ref, w_ref, o_ref, *, d_inner):
    gw = d_inner // GROUPS
    for g in range(GROUPS):
        gs = slice(g * gw, (g + 1) * gw)
        h = y_ref[:, gs] * _silu(z_ref[:, gs])
        ms = jnp.mean(h * h, axis=-1, keepdims=True)
        o_ref[:, gs] = (h * lax.rsqrt(ms + LN_EPS) * w_ref[:, gs]).astype(o_ref.dtype)


def _gnorm_call(y, zx, norm_w, rows, d_inner):
    tm = ROW_TILE
    return pl.pallas_call(
        functools.partial(_gnorm_kernel, d_inner=d_inner),
        grid=(rows // tm,),
        in_specs=[pl.BlockSpec((tm, d_inner), lambda i: (i, 0)),
                  pl.BlockSpec((tm, d_inner), lambda i: (i, 0)),
                  pl.BlockSpec((1, d_inner), lambda i: (0, 0))],
        out_specs=pl.BlockSpec((tm, d_inner), lambda i: (i, 0)),
        out_shape=jax.ShapeDtypeStruct((rows, d_inner), BF16),
        compiler_params=_cparams("parallel"),
        name="gated_rmsnorm",
    )(y, zx, norm_w)


def _conv_rows_kernel(ga_ref, gb_ref, w_ref, b_ref, o_ref, pad_ref, *, seq, n_seq):
    half = CONV_K // 2
    gap = 2 * SUBLANES
    rb = pad_ref.shape[0]
    pad_ref[:, 0:gap, :] = jnp.zeros((rb, gap, LANES), F32)
    pad_ref[:, gap + seq:gap + seq + gap, :] = jnp.zeros((rb, gap, LANES), F32)
    bias = b_ref[...]

    def body(it, carry):
        r0 = pl.multiple_of(it * (rb * seq), rb * seq)
        ga = ga_ref[pl.ds(r0, rb * seq), :]
        gb = gb_ref[pl.ds(r0, rb * seq), :]
        pad_ref[:, gap:gap + seq, :] = (ga * _sigmoid(gb)).reshape(rb, seq, LANES)
        acc = jnp.broadcast_to(bias.reshape(1, 1, LANES), (rb, seq, LANES))
        for k in range(CONV_K):
            o = gap - half + k
            acc = acc + w_ref[k:k + 1, :].reshape(1, 1, LANES) * pad_ref[:, o:o + seq, :]
        o_ref[pl.ds(r0, rb * seq), :] = acc.reshape(rb * seq, LANES)
        return carry

    lax.fori_loop(0, n_seq // rb, body, 0)


def _conv_stride_kernel(ga_ref, gb_ref, w_ref, b_ref, o_ref, pad_ref, *, n, stride):
    half = CONV_K // 2
    halo = half * stride
    pad_ref[0:halo, :] = jnp.zeros((halo, LANES), F32)
    pad_ref[halo + n:halo + n + halo, :] = jnp.zeros((halo, LANES), F32)
    tb = 256
    bias = b_ref[...]

    def fill(it, carry):
        r0 = pl.multiple_of(it * tb, tb)
        pad_ref[pl.ds(halo + r0, tb), :] = ga_ref[pl.ds(r0, tb), :] * _sigmoid(gb_ref[pl.ds(r0, tb), :])
        return carry

    lax.fori_loop(0, n // tb, fill, 0)

    def body(it, carry):
        r0 = pl.multiple_of(it * tb, tb)
        acc = jnp.broadcast_to(bias, (tb, LANES))
        for k in range(CONV_K):
            acc = acc + w_ref[k:k + 1, :] * pad_ref[pl.ds(r0 + k * stride, tb), :]
        o_ref[pl.ds(r0, tb), :] = acc
        return carry

    lax.fori_loop(0, n // tb, body, 0)


def _glu_conv_call(glu, conv_w, conv_b, out_buf, *, rows_total, row0, n, ch0, n_ch, d_conv, mode):
    assert row0 % n == 0 and ch0 % LANES == 0 and n_ch % LANES == 0
    rblk = row0 // n
    cblk = ch0 // LANES
    gblk = d_conv // LANES
    wpad = jnp.zeros((32, d_conv), F32).at[:CONV_K].set(conv_w)
    bias = conv_b.reshape(1, d_conv)
    if mode == "stride":
        kern = functools.partial(_conv_stride_kernel, n=n, stride=GRID_W)
        scratch = pltpu.VMEM((n + 2 * (CONV_K // 2) * GRID_W, LANES), F32)
    else:
        seq = GRID_W if mode == "rows" else n
        n_seq = n // seq
        rb = min(n_seq, 4)
        kern = functools.partial(_conv_rows_kernel, seq=seq, n_seq=n_seq)
        scratch = pltpu.VMEM((rb, seq + 4 * SUBLANES, LANES), F32)
    in_specs = [pl.BlockSpec((n, LANES), lambda c: (rblk, cblk + c)),
                pl.BlockSpec((n, LANES), lambda c: (rblk, gblk + cblk + c)),
                pl.BlockSpec((32, LANES), lambda c: (0, cblk + c)),
                pl.BlockSpec((1, LANES), lambda c: (0, cblk + c))]
    args = [glu, glu, wpad, bias]
    aliases = {}
    if out_buf is not None:
        in_specs.append(pl.BlockSpec(memory_space=pl.ANY))
        args.append(out_buf)
        aliases = {4: 0}
        kern = functools.partial(_drop_alias_ref, kern)
    return pl.pallas_call(
        kern,
        grid=(n_ch // LANES,),
        in_specs=in_specs,
        out_specs=pl.BlockSpec((n, LANES), lambda c: (rblk, cblk + c)),
        out_shape=jax.ShapeDtypeStruct((rows_total, d_conv), F32),
        scratch_shapes=[scratch],
        input_output_aliases=aliases,
        compiler_params=_cparams("parallel"),
        name="glu_conv_" + mode,
    )(*args)


def _drop_alias_ref(kern, ga_ref, gb_ref, w_ref, b_ref, alias_ref, o_ref, pad_ref):
    del alias_ref
    kern(ga_ref, gb_ref, w_ref, b_ref, o_ref, pad_ref)


def _lnswish_kernel(*refs, n_in):
    ins = refs[:n_in]
    g_ref, b_ref, o_ref = refs[n_in:]
    x = jnp.concatenate([r[...] for r in ins], axis=1) if n_in > 1 else ins[0][...]
    mu = jnp.mean(x, axis=-1, keepdims=True)
    xc = x - mu
    var = jnp.mean(xc * xc, axis=-1, keepdims=True)
    y = xc * lax.rsqrt(var + LN_EPS) * g_ref[...] + b_ref[...]
    o_ref[...] = _silu(y).astype(o_ref.dtype)


def _lnswish_call(parts, ln_g, ln_b):
    rows = parts[0].shape[0]
    d = sum(p.shape[1] for p in parts)
    tm = ROW_TILE
    return pl.pallas_call(
        functools.partial(_lnswish_kernel, n_in=len(parts)),
        grid=(rows // tm,),
        in_specs=[*[pl.BlockSpec((tm, p.shape[1]), lambda i: (i, 0)) for p in parts],
                  pl.BlockSpec((1, d), lambda i: (0, 0)),
                  pl.BlockSpec((1, d), lambda i: (0, 0))],
        out_specs=pl.BlockSpec((tm, d), lambda i: (i, 0)),
        out_shape=jax.ShapeDtypeStruct((rows, d), BF16),
        compiler_params=_cparams("parallel"),
        name="conv_ln_swish",
    )(*parts, ln_g.reshape(1, d), ln_b.reshape(1, d))


def _resid_ln(x, t, gate, g, b, alpha):
    v = alpha * x + gate * t
    mu = jnp.mean(v, axis=-1, keepdims=True)
    vc = v - mu
    var = jnp.mean(vc * vc, axis=-1, keepdims=True)
    return vc * lax.rsqrt(var + LN_EPS) * g + b


def _resid_ln_router_kernel(x_ref, t_ref, mod_ref, g_ref, b_ref, wr_ref, xo_ref, u_ref, aff_ref,
                            *, alpha):
    mod = mod_ref[...]
    xn = _resid_ln(x_ref[...], t_ref[...], mod[2:3, :], g_ref[...], b_ref[...], alpha)
    xo_ref[...] = xn
    u = xn * (1.0 + mod[4:5, :]) + mod[3:4, :]
    u_ref[...] = u.astype(u_ref.dtype)
    logits = jnp.dot(u, wr_ref[...], preferred_element_type=F32, precision=lax.Precision.HIGHEST)
    mx = jnp.max(logits, axis=-1, keepdims=True)
    ex = jnp.exp(logits - mx)
    aff_ref[...] = ex / jnp.sum(ex, axis=-1, keepdims=True)


def _resid_ln_kernel(x_ref, t_ref, mod_ref, g_ref, b_ref, xo_ref, *, alpha):
    mod = mod_ref[...]
    xo_ref[...] = _resid_ln(x_ref[...], t_ref[...], mod[5:6, :], g_ref[...], b_ref[...], alpha)


def _resid_call(x, t, mods, ln_g, ln_b, rows, n_lat_tiles, alpha, w_router=None):
    d = x.shape[1]
    tm = ROW_TILE
    row = pl.BlockSpec((tm, d), lambda i: (i, 0))
    vec = pl.BlockSpec((1, d), lambda i: (0, 0))
    mod_spec = pl.BlockSpec((None, N_MOD, d), lambda i: (jnp.where(i >= n_lat_tiles, 1, 0), 0, 0))
    if w_router is None:
        return pl.pallas_call(
            functools.partial(_resid_ln_kernel, alpha=alpha),
            grid=(rows // tm,),
            in_specs=[row, row, mod_spec, vec, vec],
            out_specs=row,
            out_shape=jax.ShapeDtypeStruct((rows, d), F32),
            compiler_params=_cparams("parallel"),
            name="resid_ln",
        )(x, t, mods, ln_g.reshape(1, d), ln_b.reshape(1, d))
    ne = w_router.shape[1]
    return pl.pallas_call(
        functools.partial(_resid_ln_router_kernel, alpha=alpha),
        grid=(rows // tm,),
        in_specs=[row, row, mod_spec, vec, vec, pl.BlockSpec((d, ne), lambda i: (0, 0))],
        out_specs=[row, row, pl.BlockSpec((tm, ne), lambda i: (i, 0))],
        out_shape=[jax.ShapeDtypeStruct((rows, d), F32),
                   jax.ShapeDtypeStruct((rows, d), BF16),
                   jax.ShapeDtypeStruct((rows, ne), F32)],
        compiler_params=_cparams("parallel"),
        name="resid_ln_router",
    )(x, t, mods, ln_g.reshape(1, d), ln_b.reshape(1, d), w_router)


def _experts_kernel(xe_ref, wg_ref, wu_ref, wd_ref, g_ref, o_ref, *, rc):
    f = pl.program_id(1)
    cap = xe_ref.shape[0]

    @pl.when(f == 0)
    def _():
        o_ref[...] = jnp.zeros_like(o_ref)

    wg = wg_ref[...].astype(BF16)
    wu = wu_ref[...].astype(BF16)
    wd = wd_ref[...].astype(BF16)
    for r0 in range(0, cap, rc):
        xe = xe_ref[r0:r0 + rc, :]
        hg = jnp.dot(xe, wg, preferred_element_type=F32)
        hu = jnp.dot(xe, wu, preferred_element_type=F32)
        h = (_silu(hg) * hu).astype(BF16)
        o_ref[r0:r0 + rc, :] += jnp.dot(h, wd, preferred_element_type=F32)

    @pl.when(f == pl.num_programs(1) - 1)
    def _():
        o_ref[...] = o_ref[...] * g_ref[...]


def _row_chunk(cap, limit=384, align=16):
    best = None
    for rc in range(align, min(cap, limit) + 1, align):
        if cap % rc == 0:
            best = rc
    assert best is not None, cap
    return best


def _experts_call(xe, w_gate, w_up, w_down, gsel, layer):
    ne, cap, d = xe.shape
    dexp = w_gate.shape[3]
    tf = 256 if dexp % 256 == 0 else dexp
    return pl.pallas_call(
        functools.partial(_experts_kernel, rc=_row_chunk(cap)),
        grid=(ne, dexp // tf),
        in_specs=[pl.BlockSpec((None, cap, d), lambda e, f: (e, 0, 0)),
                  pl.BlockSpec((None, None, d, tf), lambda e, f: (layer, e, 0, f)),
                  pl.BlockSpec((None, None, d, tf), lambda e, f: (layer, e, 0, f)),
                  pl.BlockSpec((None, None, tf, d), lambda e, f: (layer, e, f, 0)),
                  pl.BlockSpec((None, cap, 1), lambda e, f: (e, 0, 0))],
        out_specs=pl.BlockSpec((None, cap, d), lambda e, f: (e, 0, 0)),
        out_shape=jax.ShapeDtypeStruct((ne, cap, d), F32),
        compiler_params=_cparams("parallel", "arbitrary"),
        name="experts",
    )(xe, w_gate, w_up, w_down, gsel)


def _route(aff, n_tok):
    cap = max(1, EC_CAPACITY_FACTOR * n_tok // N_EXPERTS)
    g, idx = lax.top_k(aff.T, cap)
    return g, idx


def _spread_matrix(n_heads, direction):
    r = jnp.arange(4 * n_heads)[:, None] % (2 * n_heads)
    ch = jnp.arange(n_heads * HEAD_DIM)[None, :] // HEAD_DIM
    return (r == ch + direction * n_heads).astype(BF16)


def kernel(x, c, ctx, c_ctx, w_ada, b_ada, w_in, ssm_conv_w, ssm_conv_b, ssm_dt_bias, ssm_a_log,
           ssm_d, ssm_norm_w, w_ssm_out, conv_dw_w, conv_dw_b, conv_ln_g, conv_ln_b, w_conv_out,
           w_o, ln1_g, ln1_b, w_router, w_exp_gate, w_exp_up, w_exp_down, ln2_g, ln2_b):
    batch, n, d = x.shape
    nc = ctx.shape[1]
    depth = w_ada.shape[0]
    assert batch == 1 and n % ROW_TILE == 0 and nc % ROW_TILE == 0 and n % nc == 0
    d_inner = 2 * d
    n_heads = d_inner // HEAD_DIM
    d_bc = GROUPS * STATE
    d_xbc = d_inner + 2 * d_bc
    d_conv = d
    o_xbc = d_inner
    o_dt = o_xbc + d_xbc
    o_glu = o_dt + 2 * n_heads
    alpha = (2 * depth) ** 0.25
    nl, ncc = n // CHUNK, nc // CHUNK
    n_lat_tiles = n // ROW_TILE

    def col_tile(k_dim, width, off):
        tn = 1024
        while tn > LANES and (k_dim * tn * 4 > 8 * 1024 * 1024 or width % tn or off % tn):
            tn //= 2
        return tn

    xs = jnp.concatenate([x[0], ctx[0]], axis=0)
    cvec = jnp.zeros((SUBLANES, d), F32).at[0].set(c[0]).at[1].set(c_ctx)
    mods_all = _ada_call(cvec, w_ada, b_ada)
    e_mats = [_spread_matrix(n_heads, dr) for dr in range(2)]
    half = d_conv // 2

    for i in range(depth):
        last = i == depth - 1
        mods = mods_all[i, :2].reshape(2, N_MOD, d)
        rows_all = n + nc
        rows = n if last else rows_all
        mod_spec = pl.BlockSpec((None, N_MOD, d), _mod_spec(n_lat_tiles))

        def proj(w, layer, col_off, n_out, name, r=rows_all):
            return _mm_call(xs, w, layer, col_off, n_out, rows=r, tn=col_tile(d, n_out, col_off),
                            name=name, pro_fn=_modulate_pro, pro_specs=[mod_spec], pro_args=[mods])

        z = proj(w_in, i, 0, d_inner, "proj_z", r=rows)
        xbc = proj(w_in, i, o_xbc, d_xbc, "proj_xbc")
        dtraw = proj(w_in, i, o_dt, 2 * n_heads, "proj_dt")
        glu = proj(w_in[i][None, :, o_glu:], 0, 0, 2 * d_conv + 2 * d, "proj_glu_gate", r=rows)

        conv_w = jnp.zeros((SUBLANES, d_xbc), F32).at[:SSM_CONV].set(ssm_conv_w[i])
        conv_b = ssm_conv_b[i].reshape(1, d_xbc)
        dt_bias = ssm_dt_bias[i].reshape(1, 2 * n_heads)
        a_row = -jnp.exp(ssm_a_log[i].astype(F32)).reshape(1, 2 * n_heads)
        d_row = jnp.repeat(ssm_d[i], HEAD_DIM).reshape(1, d_inner)
        y = None
        for dr in range(2):
            y = _ssd_call(xbc, dtraw, conv_w, conv_b, dt_bias, a_row, e_mats[dr], d_row, y,
                          direction=dr, nl=nl, ncc=ncc, d_inner=d_inner)
        a_act = _gnorm_call(y, z, ssm_norm_w[i].reshape(1, d_inner), rows, d_inner)

        cw, cb = conv_dw_w[i], conv_dw_b[i]
        conv = functools.partial(_glu_conv_call, glu, cw, cb, rows_total=rows, d_conv=d_conv)
        hcv = conv(None, row0=0, n=n, ch0=0, n_ch=half, mode="rows")
        hcv = conv(hcv, row0=0, n=n, ch0=half, n_ch=d_conv - half, mode="stride")
        if not last:
            hcv = conv(hcv, row0=n, n=nc, ch0=0, n_ch=d_conv, mode="seq")
        b_act = _lnswish_call([hcv], conv_ln_g[i], conv_ln_b[i])

        def gblk(tn, off):
            return pl.BlockSpec((ROW_TILE, tn), lambda j, r: (r, j + off // tn))
        tn = col_tile(d_inner, d, 0)
        t1 = _mm_call(a_act, w_ssm_out, i, 0, d, rows=rows, tn=tn, name="ssm_out",
                      epi_fn=lambda acc, gt: _sigmoid(gt) * acc,
                      epi_specs=[gblk(tn, 2 * d_conv)], epi_args=[glu])
        tn = col_tile(d_conv, d, 0)
        t2 = _mm_call(b_act, w_conv_out, i, 0, d, rows=rows, tn=tn, name="conv_out",
                      out_dtype=BF16,
                      epi_fn=lambda acc, gt, prev: prev + _sigmoid(gt) * acc,
                      epi_specs=[gblk(tn, 2 * d_conv + d), gblk(tn, 0)], epi_args=[glu, t1])
        mix = _mm_call(t2, w_o, i, 0, d, rows=rows, tn=col_tile(d, d, 0), name="w_o")

        xs1, u, aff = _resid_call(xs, mix, mods, ln1_g[i], ln1_b[i], rows, n_lat_tiles, alpha,
                                  w_router=w_router[i])

        g_l, idx_l = _route(aff[:n], n)
        gsel, idx = g_l, idx_l
        if not last:
            g_c, idx_c = _route(aff[n:], nc)
            gsel = jnp.concatenate([g_l, g_c], axis=1)
            idx = jnp.concatenate([idx_l, idx_c + n], axis=1)
        xe = jnp.take(u, idx, axis=0)
        ye = _experts_call(xe, w_exp_gate, w_exp_up, w_exp_down, gsel[..., None], i)
        moe = jnp.zeros((rows, d), F32).at[idx.reshape(-1)].add(ye.reshape(-1, d))
        xs = _resid_call(xs1, moe, mods, ln2_g[i], ln2_b[i], rows, n_lat_tiles, alpha)

    return xs[:n].reshape(1, n, d)
```

```python
import functools

import jax
import jax.numpy as jnp
from jax import lax
from jax.experimental import pallas as pl
from jax.experimental.pallas import tpu as pltpu

F32 = jnp.float32
BF16 = jnp.bfloat16

HEAD_DIM = 64
GROUPS = 8
STATE = 128
SSM_CONV = 5
CHUNK = 128
CONV_K = 31
GRID_W = 64
N_EXPERTS = 16
EC_CAPACITY_FACTOR = 2
N_MOD = 6
LN_EPS = 1e-5

LANES = 128
SUBLANES = 8
VMEM_LIMIT = 56 * 1024 * 1024
MM_VMEM_BUDGET = 44 * 1024 * 1024
MXU_FLOPS = 1.1e15
MXU_WEIGHT_ROWS = 256
HBM_BYTES_PER_S = 3.3e12
GRID_STEP_S = 0.35e-6

ROW_TILE = 256
HALO = SUBLANES


def _cparams(*sem):
    return pltpu.CompilerParams(dimension_semantics=sem, vmem_limit_bytes=VMEM_LIMIT)


def _sigmoid(v):
    return jax.nn.sigmoid(v)


def _silu(v):
    return v * jax.nn.sigmoid(v)


def _softplus(v):
    return jnp.maximum(v, 0.0) + jnp.log1p(jnp.exp(-jnp.abs(v)))


def _ada_kernel(c_ref, w_ref, b_ref, o_ref):
    a = _silu(c_ref[...])
    o_ref[...] = jnp.dot(a, w_ref[...], preferred_element_type=F32,
                         precision=lax.Precision.HIGHEST) + b_ref[...]


def _ada_call(cvec, w_ada, b_ada):
    depth, d, nm = w_ada.shape
    tn = 1024 if nm % 1024 == 0 else nm
    return pl.pallas_call(
        _ada_kernel,
        grid=(depth, nm // tn),
        in_specs=[
            pl.BlockSpec((SUBLANES, d), lambda l, j: (0, 0)),
            pl.BlockSpec((None, d, tn), lambda l, j: (l, 0, j)),
            pl.BlockSpec((None, 1, tn), lambda l, j: (l, 0, j)),
        ],
        out_specs=pl.BlockSpec((None, SUBLANES, tn), lambda l, j: (l, 0, j)),
        out_shape=jax.ShapeDtypeStruct((depth, SUBLANES, nm), F32),
        compiler_params=_cparams("arbitrary", "arbitrary"),
        name="adaln",
    )(cvec, w_ada, b_ada.reshape(depth, 1, nm))


def _mm_kernel(*refs, n_epi, epi_fn):
    x_ref, w_ref = refs[0], refs[1]
    epi = refs[2:2 + n_epi]
    o_ref = refs[2 + n_epi]
    wb_ref = refs[3 + n_epi]

    @pl.when(pl.program_id(1) == 0)
    def _():
        wb_ref[...] = w_ref[...].astype(BF16)

    acc = jnp.dot(x_ref[...], wb_ref[...], preferred_element_type=F32)
    if epi_fn is not None:
        acc = epi_fn(acc, *[r[...] for r in epi])
    o_ref[...] = acc.astype(o_ref.dtype)


def _mm_tiles(rows, k, n_out, n_epi, out_bytes):
    best = None
    for tm in range(LANES, rows + 1, LANES):
        if rows % tm:
            continue
        for tn in (1024, 512, 256, 128):
            if n_out % tn:
                continue
            vmem = (2 * tm * k * 2 + 2 * k * tn * 4 + k * tn * 2 + 2 * tm * tn * out_bytes
                    + n_epi * 3 * tm * tn * 4 + 2 * tm * tn * 4)
            if vmem > MM_VMEM_BUDGET:
                continue
            t_mxu = 2.0 * rows * k * n_out / MXU_FLOPS * (1.0 + MXU_WEIGHT_ROWS / tm)
            t_hbm = (rows * k * 2.0 * (n_out // tn) + k * n_out * 4.0
                     + rows * n_out * (out_bytes + 4.0 * n_epi)) / HBM_BYTES_PER_S
            t = max(t_mxu, t_hbm) + (rows // tm) * (n_out // tn) * GRID_STEP_S
            if best is None or t < best[0]:
                best = (t, tm, tn)
    assert best is not None
    return best[1], best[2]


def _mm_call(x, w, layer, col_off, n_out, *, rows, name, out_dtype=F32,
             epi_fn=None, epi_cols=(), epi_args=()):
    k = x.shape[1]
    tm, tn = _mm_tiles(rows, k, n_out, len(epi_args), jnp.dtype(out_dtype).itemsize)
    assert col_off % LANES == 0
    kern = functools.partial(_mm_kernel, n_epi=len(epi_args), epi_fn=epi_fn)
    epi_specs = [pl.BlockSpec((tm, tn), functools.partial(
        lambda j, i, off: (i, j + off), off=c // tn)) for c in epi_cols]
    assert all(c % tn == 0 for c in epi_cols)
    return pl.pallas_call(
        kern,
        grid=(n_out // tn, rows // tm),
        in_specs=[pl.BlockSpec((tm, k), lambda j, i: (i, 0)),
                  pl.BlockSpec((pl.Element(k), pl.Element(tn)),
                               lambda j, i: (layer * k, pl.multiple_of(col_off + j * tn, LANES))),
                  *epi_specs],
        out_specs=pl.BlockSpec((tm, tn), lambda j, i: (i, j)),
        out_shape=jax.ShapeDtypeStruct((rows, n_out), out_dtype),
        scratch_shapes=[pltpu.VMEM((k, tn), BF16)],
        compiler_params=_cparams("arbitrary", "arbitrary"),
        name=name,
    )(x, w.reshape(-1, w.shape[-1]), *epi_args)


def _modulate_kernel(x_ref, mod_ref, o_ref):
    mod = mod_ref[...]
    o_ref[...] = (x_ref[...] * (1.0 + mod[1:2, :]) + mod[0:1, :]).astype(o_ref.dtype)


def _modulate_call(xs, mods, rows, n_lat_tiles):
    d = xs.shape[1]
    tm = ROW_TILE
    return pl.pallas_call(
        _modulate_kernel,
        grid=(rows // tm,),
        in_specs=[pl.BlockSpec((tm, d), lambda i: (i, 0)),
                  pl.BlockSpec((None, N_MOD, d),
                               lambda i: (jnp.where(i >= n_lat_tiles, 1, 0), 0, 0))],
        out_specs=pl.BlockSpec((tm, d), lambda i: (i, 0)),
        out_shape=jax.ShapeDtypeStruct((rows, d), BF16),
        compiler_params=_cparams("parallel"),
        name="modulate",
    )(xs, mods)


def _ssd_chunk_of_step(s, direction, nl, ncc):
    if direction == 0:
        return jnp.where(s < ncc, nl + s, s - ncc)
    return jnp.where(s < ncc, nl + ncc - 1 - s, nl - 1 - (s - ncc))


def _ssd_kernel(*refs, direction, nl, ncc, d_inner, has_prev):
    if has_prev:
        (xm_ref, xp_ref, xn_ref, dt_ref, cw_ref, cb_ref, dtb_ref, arow_ref, e_ref, dsk_ref,
         yprev_ref, y_ref, state_ref, ext_ref, xc_ref) = refs
    else:
        (xm_ref, xp_ref, xn_ref, dt_ref, cw_ref, cb_ref, dtb_ref, arow_ref, e_ref, dsk_ref,
         y_ref, state_ref, ext_ref, xc_ref) = refs
        yprev_ref = None
    q = CHUNK
    d_bc = GROUPS * STATE
    d_xbc = d_inner + 2 * d_bc
    gw = d_inner // GROUPS
    hpg = gw // HEAD_DIM
    n_heads = d_inner // HEAD_DIM

    s = pl.program_id(0)
    c = _ssd_chunk_of_step(s, direction, nl, ncc)

    @pl.when(s == 0)
    def _():
        state_ref[...] = jnp.zeros_like(state_ref)

    seg_first = jnp.logical_or(c == 0, c == nl)
    seg_last = jnp.logical_or(c == nl - 1, c == nl + ncc - 1)
    ext_ref[0:HALO, :] = jnp.where(seg_first, 0.0, xp_ref[...])
    ext_ref[HALO:HALO + q, :] = xm_ref[...]
    ext_ref[HALO + q:HALO + q + HALO, :] = jnp.where(seg_last, 0.0, xn_ref[...])
    ct = 512
    base = HALO - SSM_CONV // 2
    for j in range(d_xbc // ct):
        cs = slice(j * ct, (j + 1) * ct)
        acc = jnp.broadcast_to(cb_ref[:, cs], (q, ct))
        for k in range(SSM_CONV):
            acc = acc + cw_ref[k:k + 1, cs] * ext_ref[base + k:base + k + q, cs]
        xc_ref[:, cs] = _silu(acc)

    dt = _softplus(dt_ref[...] + dtb_ref[...])
    a = dt * arow_ref[...]
    ri = lax.broadcasted_iota(jnp.int32, (q, q), 0)
    ci = lax.broadcasted_iota(jnp.int32, (q, q), 1)
    causal = (ri >= ci) if direction == 0 else (ri <= ci)
    cum = jnp.dot(causal.astype(F32), a, preferred_element_type=F32,
                  precision=lax.Precision.HIGHEST)
    tot = cum[q - 1:q, :] if direction == 0 else cum[0:1, :]
    e_in = jnp.exp(cum)
    w_st = dt * jnp.exp(tot - cum)
    cum_t = cum.T
    dt_t = dt.T

    stack = jnp.concatenate([w_st, e_in], axis=0)
    hi = stack.astype(BF16)
    lo = (stack - hi.astype(F32)).astype(BF16)
    spread = jnp.dot(jnp.concatenate([hi, lo], axis=1), e_ref[...],
                     preferred_element_type=F32)
    ext_ref[0:2 * q, 0:d_inner] = spread
    dec_row = q - 1 if direction == 0 else 0

    lane = lax.broadcasted_iota(jnp.int32, (q, LANES), 1)
    first_head = lane < HEAD_DIM
    hoff = direction * n_heads

    for g in range(GROUPS):
        bg = xc_ref[:, d_inner + g * STATE:d_inner + (g + 1) * STATE]
        cg = xc_ref[:, d_inner + d_bc + g * STATE:d_inner + d_bc + (g + 1) * STATE]
        bgb = bg.astype(BF16)
        cgb = cg.astype(BF16)
        scores = lax.dot_general(cgb, bgb, (((1,), (1,)), ((), ())),
                                 preferred_element_type=F32)
        gs = slice(g * gw, (g + 1) * gw)
        st = state_ref[:, gs]
        y_g = jnp.dot(cgb, st.astype(BF16), preferred_element_type=F32) * ext_ref[q:2 * q, gs]
        xg = xc_ref[:, gs]
        xw = (xg * ext_ref[0:q, gs]).astype(BF16)
        s_new = jnp.dot(bg.T.astype(BF16), xw, preferred_element_type=F32)
        state_ref[:, gs] = st * ext_ref[q + dec_row:q + dec_row + 1, gs] + s_new
        parts = []
        for p in range(hpg // 2):
            slab = xg[:, p * LANES:(p + 1) * LANES]
            y_p = None
            for half in range(2):
                h = hoff + g * hpg + 2 * p + half
                seg = cum[:, h:h + 1] - cum_t[h:h + 1, :]
                m = scores * jnp.exp(jnp.where(causal, seg, -jnp.inf)) * dt_t[h:h + 1, :]
                keep = first_head if half == 0 else jnp.logical_not(first_head)
                xh = jnp.where(keep, slab, 0.0).astype(BF16)
                t = jnp.dot(m.astype(BF16), xh, preferred_element_type=F32)
                y_p = t if y_p is None else y_p + t
            parts.append(y_p)
        y_g = y_g + jnp.concatenate(parts, axis=1)
        if direction == 0:
            y_g = y_g + dsk_ref[:, gs] * xg
        if yprev_ref is not None:
            y_g = y_g + yprev_ref[:, gs]
        y_ref[:, gs] = y_g


def _ssd_call(xbc, dtraw, conv_w, conv_b, dt_bias, a_row, e_mat, d_row, yprev, *,
              direction, nl, ncc, d_inner):
    rows = (nl + ncc) * CHUNK
    d_xbc = d_inner + 2 * GROUPS * STATE
    nh2 = dtraw.shape[1]
    q = CHUNK
    qh = q // HALO
    cmap = functools.partial(_ssd_chunk_of_step, direction=direction, nl=nl, ncc=ncc)
    nchunks = nl + ncc
    xbc_cols = pl.BlockSpec((q, d_xbc), lambda s: (cmap(s), 0))
    prev_spec = pl.BlockSpec((HALO, d_xbc), lambda s: (jnp.maximum(cmap(s) * qh - 1, 0), 0))
    next_spec = pl.BlockSpec((HALO, d_xbc),
                             lambda s: (jnp.minimum((cmap(s) + 1) * qh, nchunks * qh - 1), 0))
    full = lambda shape: pl.BlockSpec(shape, lambda s: (0,) * len(shape))
    in_specs = [xbc_cols, prev_spec, next_spec,
                pl.BlockSpec((q, nh2), lambda s: (cmap(s), 0)),
                full((SUBLANES, d_xbc)), full((1, d_xbc)), full((1, nh2)), full((1, nh2)),
                full(e_mat.shape), full((1, d_inner))]
    args = [xbc, xbc, xbc, dtraw, conv_w, conv_b, dt_bias, a_row, e_mat, d_row]
    if yprev is not None:
        in_specs.append(pl.BlockSpec((q, d_inner), lambda s: (cmap(s), 0)))
        args.append(yprev)
    kern = functools.partial(_ssd_kernel, direction=direction, nl=nl, ncc=ncc,
                             d_inner=d_inner, has_prev=yprev is not None)
    ext_rows = max(q + 2 * HALO, 2 * q)
    return pl.pallas_call(
        kern,
        grid=(nchunks,),
        in_specs=in_specs,
        out_specs=pl.BlockSpec((q, d_inner), lambda s: (cmap(s), 0)),
        out_shape=jax.ShapeDtypeStruct((rows, d_inner), F32),
        scratch_shapes=[pltpu.VMEM((STATE, d_inner), F32),
                        pltpu.VMEM((ext_rows, d_xbc), F32),
                        pltpu.VMEM((q, d_xbc), F32)],
        compiler_params=_cparams("arbitrary"),
        name="ssd_fwd" if direction == 0 else "ssd_bwd",
    )(*args)


def _gnorm_kernel(y_ref, z_ref, w_ref, o_ref, *, d_inner):
    gw = d_inner // GROUPS
    for g in range(GROUPS):
        gs = slice(g * gw, (g + 1) * gw)
        h = y_ref[:, gs] * _silu(z_ref[:, gs])
        ms = jnp.mean(h * h, axis=-1, keepdims=True)
        o_ref[:, gs] = (h * lax.rsqrt(ms + LN_EPS) * w_ref[:, gs]).astype(o_ref.dtype)


def _gnorm_call(y, zx, norm_w, rows, d_inner):
    tm = ROW_TILE
    return pl.pallas_call(
        functools.partial(_gnorm_kernel, d_inner=d_inner),
        grid=(rows // tm,),
        in_specs=[pl.BlockSpec((tm, d_inner), lambda i: (i, 0)),
                  pl.BlockSpec((tm, d_inner), lambda i: (i, 0)),
                  pl.BlockSpec((1, d_inner), lambda i: (0, 0))],
        out_specs=pl.BlockSpec((tm, d_inner), lambda i: (i, 0)),
        out_shape=jax.ShapeDtypeStruct((rows, d_inner), BF16),
        compiler_params=_cparams("parallel"),
        name="gated_rmsnorm",
    )(y, zx, norm_w)


CONV_GAP = 2 * SUBLANES


def _conv_runs(ga_ref, gb_ref, w_ref, b_ref, o_ref, pad_ref, *, row0, seq, n_seq):
    half = CONV_K // 2
    gap = CONV_GAP
    rb = pad_ref.shape[0]
    pad_ref[:, 0:gap, :] = jnp.zeros((rb, gap, LANES), F32)
    pad_ref[:, gap + seq:gap + seq + gap, :] = jnp.zeros((rb, gap, LANES), F32)
    bias = b_ref[...]

    def body(it, carry):
        r0 = pl.multiple_of(row0 + it * (rb * seq), SUBLANES)
        ga = ga_ref[pl.ds(r0, rb * seq), :]
        gb = gb_ref[pl.ds(r0, rb * seq), :]
        pad_ref[:, gap:gap + seq, :] = (ga * _sigmoid(gb)).reshape(rb, seq, LANES)
        acc = jnp.broadcast_to(bias.reshape(1, 1, LANES), (rb, seq, LANES))
        for k in range(CONV_K):
            o = gap - half + k
            acc = acc + w_ref[k:k + 1, :].reshape(1, 1, LANES) * pad_ref[:, o:o + seq, :]
        o_ref[pl.ds(r0, rb * seq), :] = acc.reshape(rb * seq, LANES)
        return carry

    lax.fori_loop(0, n_seq // rb, body, 0)


def _conv_stride(ga_ref, gb_ref, w_ref, b_ref, o_ref, pad_ref, *, n, stride):
    half = CONV_K // 2
    halo = half * stride
    pad_ref[0:halo, :] = jnp.zeros((halo, LANES), F32)
    pad_ref[halo + n:halo + n + halo, :] = jnp.zeros((halo, LANES), F32)
    tb = 256
    bias = b_ref[...]

    def fill(it, carry):
        r0 = pl.multiple_of(it * tb, tb)
        pad_ref[pl.ds(halo + r0, tb), :] = ga_ref[pl.ds(r0, tb), :] * _sigmoid(gb_ref[pl.ds(r0, tb), :])
        return carry

    lax.fori_loop(0, n // tb, fill, 0)

    def body(it, carry):
        r0 = pl.multiple_of(it * tb, tb)
        acc = jnp.broadcast_to(bias, (tb, LANES))
        for k in range(CONV_K):
            acc = acc + w_ref[k:k + 1, :] * pad_ref[pl.ds(r0 + k * stride, tb), :]
        o_ref[pl.ds(r0, tb), :] = acc
        return carry

    lax.fori_loop(0, n // tb, body, 0)


def _glu_conv_kernel(ga_ref, gb_ref, w_ref, b_ref, o_ref, pad_rows, pad_cols, pad_ctx,
                     *, n, nc, n_row_tiles):
    c = pl.program_id(0)

    @pl.when(c < n_row_tiles)
    def _():
        _conv_runs(ga_ref, gb_ref, w_ref, b_ref, o_ref, pad_rows, row0=0, seq=GRID_W,
                   n_seq=n // GRID_W)

    @pl.when(c >= n_row_tiles)
    def _():
        _conv_stride(ga_ref, gb_ref, w_ref, b_ref, o_ref, pad_cols, n=n, stride=GRID_W)

    if nc:
        _conv_runs(ga_ref, gb_ref, w_ref, b_ref, o_ref, pad_ctx, row0=n, seq=nc, n_seq=1)


def _glu_conv_call(glu, conv_w, conv_b, *, n, nc, d_conv):
    rows = n + nc
    half = d_conv // 2
    assert half % LANES == 0 and n % (4 * GRID_W) == 0
    gblk = d_conv // LANES
    wpad = jnp.zeros((32, d_conv), F32).at[:CONV_K].set(conv_w)
    bias = conv_b.reshape(1, d_conv)
    kern = functools.partial(_glu_conv_kernel, n=n, nc=nc, n_row_tiles=half // LANES)
    return pl.pallas_call(
        kern,
        grid=(d_conv // LANES,),
        in_specs=[pl.BlockSpec((rows, LANES), lambda c: (0, c)),
                  pl.BlockSpec((rows, LANES), lambda c: (0, gblk + c)),
                  pl.BlockSpec((32, LANES), lambda c: (0, c)),
                  pl.BlockSpec((1, LANES), lambda c: (0, c))],
        out_specs=pl.BlockSpec((rows, LANES), lambda c: (0, c)),
        out_shape=jax.ShapeDtypeStruct((rows, d_conv), F32),
        scratch_shapes=[pltpu.VMEM((4, GRID_W + 2 * CONV_GAP, LANES), F32),
                        pltpu.VMEM((n + 2 * (CONV_K // 2) * GRID_W, LANES), F32),
                        pltpu.VMEM((1, max(nc, SUBLANES) + 2 * CONV_GAP, LANES), F32)],
        compiler_params=_cparams("parallel"),
        name="glu_conv",
    )(glu, glu, wpad, bias)


def _lnswish_kernel(*refs, n_in):
    ins = refs[:n_in]
    g_ref, b_ref, o_ref = refs[n_in:]
    x = jnp.concatenate([r[...] for r in ins], axis=1) if n_in > 1 else ins[0][...]
    mu = jnp.mean(x, axis=-1, keepdims=True)
    xc = x - mu
    var = jnp.mean(xc * xc, axis=-1, keepdims=True)
    y = xc * lax.rsqrt(var + LN_EPS) * g_ref[...] + b_ref[...]
    o_ref[...] = _silu(y).astype(o_ref.dtype)


def _lnswish_call(parts, ln_g, ln_b):
    rows = parts[0].shape[0]
    d = sum(p.shape[1] for p in parts)
    tm = ROW_TILE
    return pl.pallas_call(
        functools.partial(_lnswish_kernel, n_in=len(parts)),
        grid=(rows // tm,),
        in_specs=[*[pl.BlockSpec((tm, p.shape[1]), lambda i: (i, 0)) for p in parts],
                  pl.BlockSpec((1, d), lambda i: (0, 0)),
                  pl.BlockSpec((1, d), lambda i: (0, 0))],
        out_specs=pl.BlockSpec((tm, d), lambda i: (i, 0)),
        out_shape=jax.ShapeDtypeStruct((rows, d), BF16),
        compiler_params=_cparams("parallel"),
        name="conv_ln_swish",
    )(*parts, ln_g.reshape(1, d), ln_b.reshape(1, d))


def _resid_ln(x, t, gate, g, b, alpha):
    v = alpha * x + gate * t
    mu = jnp.mean(v, axis=-1, keepdims=True)
    vc = v - mu
    var = jnp.mean(vc * vc, axis=-1, keepdims=True)
    return vc * lax.rsqrt(var + LN_EPS) * g + b


def _resid_ln_router_kernel(x_ref, t_ref, mod_ref, g_ref, b_ref, wr_ref, xo_ref, aff_ref,
                            *, alpha):
    mod = mod_ref[...]
    xn = _resid_ln(x_ref[...], t_ref[...], mod[2:3, :], g_ref[...], b_ref[...], alpha)
    xo_ref[...] = xn
    u = xn * (1.0 + mod[4:5, :]) + mod[3:4, :]
    logits = jnp.dot(u, wr_ref[...], preferred_element_type=F32, precision=lax.Precision.HIGHEST)
    mx = jnp.max(logits, axis=-1, keepdims=True)
    ex = jnp.exp(logits - mx)
    aff_ref[...] = ex / jnp.sum(ex, axis=-1, keepdims=True)


def _resid_ln_kernel(x_ref, t_ref, mod_ref, g_ref, b_ref, xo_ref, *, alpha):
    mod = mod_ref[...]
    xo_ref[...] = _resid_ln(x_ref[...], t_ref[...], mod[5:6, :], g_ref[...], b_ref[...], alpha)


def _resid_call(x, t, mods, ln_g, ln_b, rows, n_lat_tiles, alpha, w_router=None):
    d = x.shape[1]
    tm = ROW_TILE
    row = pl.BlockSpec((tm, d), lambda i: (i, 0))
    vec = pl.BlockSpec((1, d), lambda i: (0, 0))
    mod_spec = pl.BlockSpec((None, N_MOD, d), lambda i: (jnp.where(i >= n_lat_tiles, 1, 0), 0, 0))
    if w_router is None:
        return pl.pallas_call(
            functools.partial(_resid_ln_kernel, alpha=alpha),
            grid=(rows // tm,),
            in_specs=[row, row, mod_spec, vec, vec],
            out_specs=row,
            out_shape=jax.ShapeDtypeStruct((rows, d), F32),
            compiler_params=_cparams("parallel"),
            name="resid_ln",
        )(x, t, mods, ln_g.reshape(1, d), ln_b.reshape(1, d))
    ne = w_router.shape[1]
    return pl.pallas_call(
        functools.partial(_resid_ln_router_kernel, alpha=alpha),
        grid=(rows // tm,),
        in_specs=[row, row, mod_spec, vec, vec, pl.BlockSpec((d, ne), lambda i: (0, 0))],
        out_specs=[row, pl.BlockSpec((tm, ne), lambda i: (i, 0))],
        out_shape=[jax.ShapeDtypeStruct((rows, d), F32),
                   jax.ShapeDtypeStruct((rows, ne), F32)],
        compiler_params=_cparams("parallel"),
        name="resid_ln_router",
    )(x, t, mods, ln_g.reshape(1, d), ln_b.reshape(1, d), w_router)


def _experts_kernel(idx_ref, x_hbm, mod_ref, wg_ref, wu_ref, wd_ref, g_ref, o_ref,
                    rows_ref, xe_ref, sem, *, rc, cap_lat):
    e = pl.program_id(0)
    f = pl.program_id(1)
    cap = xe_ref.shape[0]

    def start_gather(ee):
        def issue(r, carry):
            t = idx_ref[ee, r]
            pltpu.make_async_copy(x_hbm.at[pl.ds(t, 1), :], rows_ref.at[pl.ds(r, 1), :], sem).start()
            return carry
        lax.fori_loop(0, cap, issue, 0, unroll=8)

    @pl.when(jnp.logical_and(e == 0, f == 0))
    def _():
        start_gather(0)

    @pl.when(f == 0)
    def _():
        pltpu.make_async_copy(rows_ref, rows_ref, sem).wait()
        mod = mod_ref[...]
        for lo, hi, m in ((0, cap_lat, 0), (cap_lat, cap, 1)):
            if hi > lo:
                sh, sc = mod[m, 3:4, :], mod[m, 4:5, :]
                xe_ref[lo:hi, :] = (rows_ref[lo:hi, :] * (1.0 + sc) + sh).astype(BF16)
        o_ref[...] = jnp.zeros_like(o_ref)

        @pl.when(e + 1 < pl.num_programs(0))
        def _():
            start_gather(e + 1)

    wg = wg_ref[...].astype(BF16)
    wu = wu_ref[...].astype(BF16)
    wd = wd_ref[...].astype(BF16)
    for r0 in range(0, cap, rc):
        xe = xe_ref[r0:r0 + rc, :]
        hg = jnp.dot(xe, wg, preferred_element_type=F32)
        hu = jnp.dot(xe, wu, preferred_element_type=F32)
        h = (_silu(hg) * hu).astype(BF16)
        o_ref[r0:r0 + rc, :] += jnp.dot(h, wd, preferred_element_type=F32)

    @pl.when(f == pl.num_programs(1) - 1)
    def _():
        o_ref[...] = o_ref[...] * g_ref[...]


def _row_chunk(cap, limit=384, align=16):
    best = None
    for rc in range(align, min(cap, limit) + 1, align):
        if cap % rc == 0:
            best = rc
    assert best is not None, cap
    return best


def _experts_call(idx, xs, mods, w_gate, w_up, w_down, gsel, layer, cap_lat):
    ne, cap = idx.shape
    d = xs.shape[1]
    dexp = w_gate.shape[3]
    tf = 256 if dexp % 256 == 0 else dexp
    grid_spec = pltpu.PrefetchScalarGridSpec(
        num_scalar_prefetch=1,
        grid=(ne, dexp // tf),
        in_specs=[pl.BlockSpec(memory_space=pl.ANY),
                  pl.BlockSpec((2, N_MOD, d), lambda e, f, idx: (0, 0, 0)),
                  pl.BlockSpec((None, None, d, tf), lambda e, f, idx: (layer, e, 0, f)),
                  pl.BlockSpec((None, None, d, tf), lambda e, f, idx: (layer, e, 0, f)),
                  pl.BlockSpec((None, None, tf, d), lambda e, f, idx: (layer, e, f, 0)),
                  pl.BlockSpec((None, cap, 1), lambda e, f, idx: (e, 0, 0))],
        out_specs=pl.BlockSpec((None, cap, d), lambda e, f, idx: (e, 0, 0)),
        scratch_shapes=[pltpu.VMEM((cap, d), F32), pltpu.VMEM((cap, d), BF16),
                        pltpu.SemaphoreType.DMA(())],
    )
    return pl.pallas_call(
        functools.partial(_experts_kernel, rc=_row_chunk(cap), cap_lat=cap_lat),
        grid_spec=grid_spec,
        out_shape=jax.ShapeDtypeStruct((ne, cap, d), F32),
        compiler_params=_cparams("arbitrary", "arbitrary"),
        name="experts",
    )(idx, xs, mods, w_gate, w_up, w_down, gsel)


def _route(aff, n_tok):
    cap = max(1, EC_CAPACITY_FACTOR * n_tok // N_EXPERTS)
    g, idx = lax.top_k(aff.T, cap)
    return g, idx


def _spread_matrix(n_heads, direction):
    r = jnp.arange(4 * n_heads)[:, None] % (2 * n_heads)
    ch = jnp.arange(n_heads * HEAD_DIM)[None, :] // HEAD_DIM
    return (r == ch + direction * n_heads).astype(BF16)


def kernel(x, c, ctx, c_ctx, w_ada, b_ada, w_in, ssm_conv_w, ssm_conv_b, ssm_dt_bias, ssm_a_log,
           ssm_d, ssm_norm_w, w_ssm_out, conv_dw_w, conv_dw_b, conv_ln_g, conv_ln_b, w_conv_out,
           w_o, ln1_g, ln1_b, w_router, w_exp_gate, w_exp_up, w_exp_down, ln2_g, ln2_b):
    batch, n, d = x.shape
    nc = ctx.shape[1]
    depth = w_ada.shape[0]
    assert batch == 1 and n % ROW_TILE == 0 and nc % ROW_TILE == 0 and n % nc == 0
    d_inner = 2 * d
    n_heads = d_inner // HEAD_DIM
    d_bc = GROUPS * STATE
    d_xbc = d_inner + 2 * d_bc
    d_conv = d
    o_xbc = d_inner
    o_dt = o_xbc + d_xbc
    o_glu = o_dt + 2 * n_heads
    alpha = (2 * depth) ** 0.25
    nl, ncc = n // CHUNK, nc // CHUNK
    n_lat_tiles = n // ROW_TILE

    xs = jnp.concatenate([x[0], ctx[0]], axis=0)
    cvec = jnp.zeros((SUBLANES, d), F32).at[0].set(c[0]).at[1].set(c_ctx)
    mods_all = _ada_call(cvec, w_ada, b_ada)
    e_mats = [_spread_matrix(n_heads, dr) for dr in range(2)]

    for i in range(depth):
        last = i == depth - 1
        mods = mods_all[i, :2].reshape(2, N_MOD, d)
        rows_all = n + nc
        rows = n if last else rows_all

        u_in = _modulate_call(xs, mods, rows_all, n_lat_tiles)
        z = _mm_call(u_in, w_in, i, 0, d_inner, rows=rows, name="proj_z")
        xbc = _mm_call(u_in, w_in, i, o_xbc, d_xbc, rows=rows_all, name="proj_xbc")
        dtraw = _mm_call(u_in, w_in, i, o_dt, 2 * n_heads, rows=rows_all, name="proj_dt")
        glu = _mm_call(u_in, w_in, i, o_glu, 2 * d_conv + 2 * d, rows=rows, name="proj_glu_gate")

        conv_w = jnp.zeros((SUBLANES, d_xbc), F32).at[:SSM_CONV].set(ssm_conv_w[i])
        conv_b = ssm_conv_b[i].reshape(1, d_xbc)
        dt_bias = ssm_dt_bias[i].reshape(1, 2 * n_heads)
        a_row = -jnp.exp(ssm_a_log[i].astype(F32)).reshape(1, 2 * n_heads)
        d_row = jnp.repeat(ssm_d[i], HEAD_DIM).reshape(1, d_inner)
        y = None
        for dr in range(2):
            y = _ssd_call(xbc, dtraw, conv_w, conv_b, dt_bias, a_row, e_mats[dr], d_row, y,
                          direction=dr, nl=nl, ncc=ncc, d_inner=d_inner)
        a_act = _gnorm_call(y, z, ssm_norm_w[i].reshape(1, d_inner), rows, d_inner)

        hcv = _glu_conv_call(glu, conv_dw_w[i], conv_dw_b[i], n=n, nc=rows - n, d_conv=d_conv)
        b_act = _lnswish_call([hcv], conv_ln_g[i], conv_ln_b[i])

        t1 = _mm_call(a_act, w_ssm_out, i, 0, d, rows=rows, name="ssm_out",
                      epi_fn=lambda acc, gt: _sigmoid(gt) * acc,
                      epi_cols=[2 * d_conv], epi_args=[glu])
        t2 = _mm_call(b_act, w_conv_out, i, 0, d, rows=rows, name="conv_out", out_dtype=BF16,
                      epi_fn=lambda acc, gt, prev: prev + _sigmoid(gt) * acc,
                      epi_cols=[2 * d_conv + d, 0], epi_args=[glu, t1])
        mix = _mm_call(t2, w_o, i, 0, d, rows=rows, name="w_o")

        xs1, aff = _resid_call(xs, mix, mods, ln1_g[i], ln1_b[i], rows, n_lat_tiles, alpha,
                               w_router=w_router[i])

        g_l, idx_l = _route(aff[:n], n)
        gsel, idx = g_l, idx_l
        if not last:
            g_c, idx_c = _route(aff[n:], nc)
            gsel = jnp.concatenate([g_l, g_c], axis=1)
            idx = jnp.concatenate([idx_l, idx_c + n], axis=1)
        ye = _experts_call(idx, xs1, mods, w_exp_gate, w_exp_up, w_exp_down, gsel[..., None], i,
                           idx_l.shape[1])
        moe = jnp.zeros((rows, d), F32).at[idx.reshape(-1)].add(ye.reshape(-1, d))
        xs = _resid_call(xs1, moe, mods, ln2_g[i], ln2_b[i], rows, n_lat_tiles, alpha)

    return xs[:n].reshape(1, n, d)
```

```python
import functools

import jax
import jax.numpy as jnp
from jax import lax
from jax.experimental import pallas as pl
from jax.experimental.pallas import tpu as pltpu

F32 = jnp.float32
BF16 = jnp.bfloat16

HEAD_DIM = 64
GROUPS = 8
STATE = 128
SSM_CONV = 5
CHUNK = 128
CONV_K = 31
GRID_W = 64
N_EXPERTS = 16
EC_CAPACITY_FACTOR = 2
N_MOD = 6
LN_EPS = 1e-5

LANES = 128
SUBLANES = 8
VMEM_LIMIT = 56 * 1024 * 1024
MM_VMEM_BUDGET = 44 * 1024 * 1024
MXU_FLOPS = 1.1e15
MXU_WEIGHT_ROWS = 256
HBM_BYTES_PER_S = 3.3e12
GRID_STEP_S = 0.35e-6

ROW_TILE = 256
HALO = SUBLANES


def _cparams(*sem):
    return pltpu.CompilerParams(dimension_semantics=sem, vmem_limit_bytes=VMEM_LIMIT)


def _sigmoid(v):
    return jax.nn.sigmoid(v)


def _silu(v):
    return v * jax.nn.sigmoid(v)


def _softplus(v):
    return jnp.maximum(v, 0.0) + jnp.log1p(jnp.exp(-jnp.abs(v)))


def _ada_kernel(c_ref, w_ref, b_ref, o_ref):
    a = _silu(c_ref[...])
    o_ref[...] = jnp.dot(a, w_ref[...], preferred_element_type=F32,
                         precision=lax.Precision.HIGHEST) + b_ref[...]


def _ada_call(cvec, w_ada, b_ada):
    depth, d, nm = w_ada.shape
    tn = 1024 if nm % 1024 == 0 else nm
    return pl.pallas_call(
        _ada_kernel,
        grid=(depth, nm // tn),
        in_specs=[
            pl.BlockSpec((SUBLANES, d), lambda l, j: (0, 0)),
            pl.BlockSpec((None, d, tn), lambda l, j: (l, 0, j)),
            pl.BlockSpec((None, 1, tn), lambda l, j: (l, 0, j)),
        ],
        out_specs=pl.BlockSpec((None, SUBLANES, tn), lambda l, j: (l, 0, j)),
        out_shape=jax.ShapeDtypeStruct((depth, SUBLANES, nm), F32),
        compiler_params=_cparams("arbitrary", "arbitrary"),
        name="adaln",
    )(cvec, w_ada, b_ada.reshape(depth, 1, nm))


def _mm_kernel(*refs, n_epi, epi_fn):
    x_ref, w_ref = refs[0], refs[1]
    epi = refs[2:2 + n_epi]
    o_ref = refs[2 + n_epi]
    wb_ref = refs[3 + n_epi]

    @pl.when(pl.program_id(1) == 0)
    def _():
        wb_ref[...] = w_ref[...].astype(BF16)

    acc = jnp.dot(x_ref[...], wb_ref[...], preferred_element_type=F32)
    if epi_fn is not None:
        acc = epi_fn(acc, *[r[...] for r in epi])
    if len(o_ref.shape) == 3:
        for jj in range(o_ref.shape[0]):
            o_ref[jj] = acc[:, jj * LANES:(jj + 1) * LANES].astype(o_ref.dtype)
    else:
        o_ref[...] = acc.astype(o_ref.dtype)


def _mm_tiles(rows, k, n_out, n_epi, out_bytes):
    best = None
    for tm in range(LANES, rows + 1, LANES):
        if rows % tm:
            continue
        for tn in (1024, 512, 256, 128):
            if n_out % tn:
                continue
            vmem = (2 * tm * k * 2 + 2 * k * tn * 4 + k * tn * 2 + 2 * tm * tn * out_bytes
                    + n_epi * 3 * tm * tn * 4 + 2 * tm * tn * 4)
            if vmem > MM_VMEM_BUDGET:
                continue
            t_mxu = 2.0 * rows * k * n_out / MXU_FLOPS * (1.0 + MXU_WEIGHT_ROWS / tm)
            t_hbm = (rows * k * 2.0 * (n_out // tn) + k * n_out * 4.0
                     + rows * n_out * (out_bytes + 4.0 * n_epi)) / HBM_BYTES_PER_S
            t = max(t_mxu, t_hbm) + (rows // tm) * (n_out // tn) * GRID_STEP_S
            if best is None or t < best[0]:
                best = (t, tm, tn)
    assert best is not None
    return best[1], best[2]


def _mm_call(x, w, layer, col_off, n_out, *, rows, name, out_dtype=F32,
             epi_fn=None, epi_cols=(), epi_args=(), slab_major=False):
    k = x.shape[1]
    tm, tn = _mm_tiles(rows, k, n_out, len(epi_args), jnp.dtype(out_dtype).itemsize)
    assert col_off % LANES == 0
    kern = functools.partial(_mm_kernel, n_epi=len(epi_args), epi_fn=epi_fn)
    epi_specs = [pl.BlockSpec((tm, tn), functools.partial(
        lambda j, i, off: (i, j + off), off=c // tn)) for c in epi_cols]
    assert all(c % tn == 0 for c in epi_cols)
    if slab_major:
        out_spec = pl.BlockSpec((tn // LANES, tm, LANES), lambda j, i: (j, i, 0))
        out_shape = jax.ShapeDtypeStruct((n_out // LANES, rows, LANES), out_dtype)
    else:
        out_spec = pl.BlockSpec((tm, tn), lambda j, i: (i, j))
        out_shape = jax.ShapeDtypeStruct((rows, n_out), out_dtype)
    return pl.pallas_call(
        kern,
        grid=(n_out // tn, rows // tm),
        in_specs=[pl.BlockSpec((tm, k), lambda j, i: (i, 0)),
                  pl.BlockSpec((pl.Element(k), pl.Element(tn)),
                               lambda j, i: (layer * k, pl.multiple_of(col_off + j * tn, LANES))),
                  *epi_specs],
        out_specs=out_spec,
        out_shape=out_shape,
        scratch_shapes=[pltpu.VMEM((k, tn), BF16)],
        compiler_params=_cparams("arbitrary", "arbitrary"),
        name=name,
    )(x, w.reshape(-1, w.shape[-1]), *epi_args)


def _modulate_kernel(x_ref, mod_ref, o_ref):
    mod = mod_ref[...]
    o_ref[...] = (x_ref[...] * (1.0 + mod[1:2, :]) + mod[0:1, :]).astype(o_ref.dtype)


def _modulate_call(xs, mods, rows, n_lat_tiles):
    d = xs.shape[1]
    tm = ROW_TILE
    return pl.pallas_call(
        _modulate_kernel,
        grid=(rows // tm,),
        in_specs=[pl.BlockSpec((tm, d), lambda i: (i, 0)),
                  pl.BlockSpec((None, N_MOD, d),
                               lambda i: (jnp.where(i >= n_lat_tiles, 1, 0), 0, 0))],
        out_specs=pl.BlockSpec((tm, d), lambda i: (i, 0)),
        out_shape=jax.ShapeDtypeStruct((rows, d), BF16),
        compiler_params=_cparams("parallel"),
        name="modulate",
    )(xs, mods)


def _ssd_chunk_of_step(s, direction, nl, ncc):
    if direction == 0:
        return jnp.where(s < ncc, nl + s, s - ncc)
    return jnp.where(s < ncc, nl + ncc - 1 - s, nl - 1 - (s - ncc))


def _ssd_kernel(*refs, direction, nl, ncc, d_inner):
    if direction == 0:
        (xm_ref, xp_ref, xn_ref, dt_ref, cw_ref, cb_ref, dtb_ref, arow_ref, e_ref, dsk_ref,
         y_ref, xc_ref, state_ref, ext_ref) = refs
        yprev_ref = None
    else:
        (xc_ref, dt_ref, dtb_ref, arow_ref, e_ref, yprev_ref, y_ref, state_ref, ext_ref,
         ys_ref) = refs
    q = CHUNK
    d_bc = GROUPS * STATE
    d_xbc = d_inner + 2 * d_bc
    gw = d_inner // GROUPS
    hpg = gw // HEAD_DIM
    n_heads = d_inner // HEAD_DIM
    n_grp = q // SUBLANES

    s = pl.program_id(0)
    c = _ssd_chunk_of_step(s, direction, nl, ncc)

    @pl.when(s == 0)
    def _():
        state_ref[...] = jnp.zeros_like(state_ref)

    def slabs(ref3, rows):
        return jnp.concatenate([ref3[j, rows, :] for j in range(ref3.shape[0])], axis=1)

    if direction == 0:
        pad = SSM_CONV // 2
        seg_first = jnp.logical_or(c == 0, c == nl)
        seg_last = jnp.logical_or(c == nl - 1, c == nl + ncc - 1)
        every = slice(None)
        prev = jnp.where(seg_first, 0.0, slabs(xp_ref, every))
        nxt = jnp.where(seg_last, 0.0, slabs(xn_ref, every))
        sub = lax.broadcasted_iota(jnp.int32, (SUBLANES, d_xbc), 0)
        for m in range(n_grp):
            grp = slabs(xm_ref, pl.ds(m, SUBLANES, stride=n_grp))
            ext_ref[(pad + m) * SUBLANES:(pad + m + 1) * SUBLANES, :] = grp
            if m >= n_grp - pad:
                j = m - (n_grp - pad)
                halo = prev[HALO - pad + j:HALO - pad + j + 1, :]
                ext_ref[j * SUBLANES:(j + 1) * SUBLANES, :] = jnp.where(
                    sub == 0, halo, pltpu.roll(grp, 1, axis=0))
            if m < pad:
                halo = nxt[m:m + 1, :]
                ext_ref[(pad + n_grp + m) * SUBLANES:(pad + n_grp + m + 1) * SUBLANES, :] = (
                    jnp.where(sub == SUBLANES - 1, halo, pltpu.roll(grp, SUBLANES - 1, axis=0)))
        ct = 512
        for j in range(d_xbc // ct):
            cs = slice(j * ct, (j + 1) * ct)
            acc = jnp.broadcast_to(cb_ref[:, cs], (q, ct))
            for k in range(SSM_CONV):
                acc = acc + cw_ref[k:k + 1, cs] * ext_ref[k * SUBLANES:k * SUBLANES + q, cs]
            xc_ref[:, cs] = _silu(acc)

    dt_raw = jnp.concatenate([dt_ref[pl.ds(m, SUBLANES, stride=n_grp), :] for m in range(n_grp)],
                             axis=0)
    dt = _softplus(dt_raw + dtb_ref[...])
    a = dt * arow_ref[...]
    ri = lax.broadcasted_iota(jnp.int32, (q, q), 0)
    ci = lax.broadcasted_iota(jnp.int32, (q, q), 1)
    tok_i = (ri % SUBLANES) * n_grp + ri // SUBLANES
    tok_j = (ci % SUBLANES) * n_grp + ci // SUBLANES
    causal = (tok_i >= tok_j) if direction == 0 else (tok_i <= tok_j)
    cum = jnp.dot(causal.astype(F32), a, preferred_element_type=F32,
                  precision=lax.Precision.HIGHEST)
    tot = cum[q - 1:q, :] if direction == 0 else cum[0:1, :]
    e_in = jnp.exp(cum)
    w_st = dt * jnp.exp(tot - cum)
    cum_t = cum.T
    dt_t = dt.T

    stack = jnp.concatenate([w_st, e_in], axis=0)
    hi = stack.astype(BF16)
    lo = (stack - hi.astype(F32)).astype(BF16)
    spread = jnp.dot(jnp.concatenate([hi, lo], axis=1), e_ref[...],
                     preferred_element_type=F32)
    ext_ref[0:2 * q, 0:d_inner] = spread
    dec_row = q - 1 if direction == 0 else 0

    lane = lax.broadcasted_iota(jnp.int32, (q, LANES), 1)
    first_head = lane < HEAD_DIM
    hoff = direction * n_heads

    for g in range(GROUPS):
        bg = xc_ref[:, d_inner + g * STATE:d_inner + (g + 1) * STATE]
        cg = xc_ref[:, d_inner + d_bc + g * STATE:d_inner + d_bc + (g + 1) * STATE]
        bgb = bg.astype(BF16)
        cgb = cg.astype(BF16)
        scores = lax.dot_general(cgb, bgb, (((1,), (1,)), ((), ())),
                                 preferred_element_type=F32)
        gs = slice(g * gw, (g + 1) * gw)
        st = state_ref[:, gs]
        y_g = jnp.dot(cgb, st.astype(BF16), preferred_element_type=F32) * ext_ref[q:2 * q, gs]
        xg = xc_ref[:, gs]
        xw = (xg * ext_ref[0:q, gs]).astype(BF16)
        s_new = jnp.dot(bg.T.astype(BF16), xw, preferred_element_type=F32)
        state_ref[:, gs] = st * ext_ref[q + dec_row:q + dec_row + 1, gs] + s_new
        parts = []
        for p in range(hpg // 2):
            slab = xg[:, p * LANES:(p + 1) * LANES]
            y_p = None
            for half in range(2):
                h = hoff + g * hpg + 2 * p + half
                seg = cum[:, h:h + 1] - cum_t[h:h + 1, :]
                m = scores * jnp.exp(jnp.where(causal, seg, -jnp.inf)) * dt_t[h:h + 1, :]
                keep = first_head if half == 0 else jnp.logical_not(first_head)
                xh = jnp.where(keep, slab, 0.0).astype(BF16)
                t = jnp.dot(m.astype(BF16), xh, preferred_element_type=F32)
                y_p = t if y_p is None else y_p + t
            parts.append(y_p)
        y_g = y_g + jnp.concatenate(parts, axis=1)
        if direction == 0:
            y_ref[:, gs] = y_g + dsk_ref[:, gs] * xg
        else:
            y_g = y_g + yprev_ref[:, gs]
            for jl in range(gw // LANES):
                ys_ref[...] = y_g[:, jl * LANES:(jl + 1) * LANES]
                jg = g * (gw // LANES) + jl
                for a in range(n_grp):
                    start = (a % 2) * (q // 2) + a // 2
                    y_ref[jg, a * SUBLANES:(a + 1) * SUBLANES, :] = (
                        ys_ref[pl.ds(start, SUBLANES, stride=SUBLANES), :])


def _ssd_call(xbc, dtraw, conv_w, conv_b, dt_bias, a_row, e_mats, d_row, *, nl, ncc, d_inner):
    rows = (nl + ncc) * CHUNK
    d_xbc = d_inner + 2 * GROUPS * STATE
    nh2 = dtraw.shape[1]
    q = CHUNK
    qh = q // HALO
    nchunks = nl + ncc
    full = lambda shape: pl.BlockSpec(shape, lambda s: (0,) * len(shape))
    scratch = [pltpu.VMEM((STATE, d_inner), F32), pltpu.VMEM((2 * q, d_xbc), F32)]

    def chunk_spec(width, direction):
        cmap = functools.partial(_ssd_chunk_of_step, direction=direction, nl=nl, ncc=ncc)
        return pl.BlockSpec((q, width), lambda s: (cmap(s), 0))

    cmap = functools.partial(_ssd_chunk_of_step, direction=0, nl=nl, ncc=ncc)
    cmap_b = functools.partial(_ssd_chunk_of_step, direction=1, nl=nl, ncc=ncc)
    ns_x, ns_y = d_xbc // LANES, d_inner // LANES
    main_spec = pl.BlockSpec((ns_x, q, LANES), lambda s: (0, cmap(s), 0))
    prev_spec = pl.BlockSpec((ns_x, HALO, LANES),
                             lambda s: (0, jnp.maximum(cmap(s) * qh - 1, 0), 0))
    next_spec = pl.BlockSpec((ns_x, HALO, LANES),
                             lambda s: (0, jnp.minimum((cmap(s) + 1) * qh, nchunks * qh - 1), 0))
    y_fwd, xc = pl.pallas_call(
        functools.partial(_ssd_kernel, direction=0, nl=nl, ncc=ncc, d_inner=d_inner),
        grid=(nchunks,),
        in_specs=[main_spec, prev_spec, next_spec, chunk_spec(nh2, 0),
                  full((SUBLANES, d_xbc)), full((1, d_xbc)), full((1, nh2)), full((1, nh2)),
                  full(e_mats[0].shape), full((1, d_inner))],
        out_specs=[chunk_spec(d_inner, 0), chunk_spec(d_xbc, 0)],
        out_shape=[jax.ShapeDtypeStruct((rows, d_inner), F32),
                   jax.ShapeDtypeStruct((rows, d_xbc), F32)],
        scratch_shapes=scratch,
        compiler_params=_cparams("arbitrary"),
        name="ssd_fwd",
    )(xbc, xbc, xbc, dtraw, conv_w, conv_b, dt_bias, a_row, e_mats[0], d_row)
    return pl.pallas_call(
        functools.partial(_ssd_kernel, direction=1, nl=nl, ncc=ncc, d_inner=d_inner),
        grid=(nchunks,),
        in_specs=[chunk_spec(d_xbc, 1), chunk_spec(nh2, 1), full((1, nh2)), full((1, nh2)),
                  full(e_mats[1].shape), chunk_spec(d_inner, 1)],
        out_specs=pl.BlockSpec((ns_y, q, LANES), lambda s: (0, cmap_b(s), 0)),
        out_shape=jax.ShapeDtypeStruct((ns_y, rows, LANES), F32),
        scratch_shapes=scratch + [pltpu.VMEM((q, LANES), F32)],
        compiler_params=_cparams("arbitrary"),
        name="ssd_bwd",
    )(xc, dtraw, dt_bias, a_row, e_mats[1], y_fwd)


def _gnorm_kernel(y_ref, z_ref, w_ref, o_ref, *, d_inner):
    gw = d_inner // GROUPS
    spg = gw // LANES
    for g in range(GROUPS):
        gs = slice(g * gw, (g + 1) * gw)
        y = jnp.concatenate([y_ref[g * spg + j] for j in range(spg)], axis=1)
        h = y * _silu(z_ref[:, gs])
        ms = jnp.mean(h * h, axis=-1, keepdims=True)
        o_ref[:, gs] = (h * lax.rsqrt(ms + LN_EPS) * w_ref[:, gs]).astype(o_ref.dtype)


def _gnorm_call(y, zx, norm_w, rows, d_inner):
    tm = ROW_TILE
    return pl.pallas_call(
        functools.partial(_gnorm_kernel, d_inner=d_inner),
        grid=(rows // tm,),
        in_specs=[pl.BlockSpec((d_inner // LANES, tm, LANES), lambda i: (0, i, 0)),
                  pl.BlockSpec((tm, d_inner), lambda i: (i, 0)),
                  pl.BlockSpec((1, d_inner), lambda i: (0, 0))],
        out_specs=pl.BlockSpec((tm, d_inner), lambda i: (i, 0)),
        out_shape=jax.ShapeDtypeStruct((rows, d_inner), BF16),
        compiler_params=_cparams("parallel"),
        name="gated_rmsnorm",
    )(y, zx, norm_w)


CONV_GAP = 2 * SUBLANES


def _conv_runs(ga_ref, gb_ref, w_ref, b_ref, o_ref, pad_ref, *, row0, seq, n_seq):
    half = CONV_K // 2
    gap = CONV_GAP
    rb = pad_ref.shape[0]
    pad_ref[:, 0:gap, :] = jnp.zeros((rb, gap, LANES), F32)
    pad_ref[:, gap + seq:gap + seq + gap, :] = jnp.zeros((rb, gap, LANES), F32)
    bias = b_ref[...]

    def body(it, carry):
        r0 = pl.multiple_of(row0 + it * (rb * seq), SUBLANES)
        ga = ga_ref[pl.ds(r0, rb * seq), :]
        gb = gb_ref[pl.ds(r0, rb * seq), :]
        pad_ref[:, gap:gap + seq, :] = (ga * _sigmoid(gb)).reshape(rb, seq, LANES)
        acc = jnp.broadcast_to(bias.reshape(1, 1, LANES), (rb, seq, LANES))
        for k in range(CONV_K):
            o = gap - half + k
            acc = acc + w_ref[k:k + 1, :].reshape(1, 1, LANES) * pad_ref[:, o:o + seq, :]
        o_ref[pl.ds(r0, rb * seq), :] = acc.reshape(rb * seq, LANES)
        return carry

    lax.fori_loop(0, n_seq // rb, body, 0)


def _conv_stride(ga_ref, gb_ref, w_ref, b_ref, o_ref, pad_ref, *, n, stride):
    half = CONV_K // 2
    halo = half * stride
    pad_ref[0:halo, :] = jnp.zeros((halo, LANES), F32)
    pad_ref[halo + n:halo + n + halo, :] = jnp.zeros((halo, LANES), F32)
    tb = 256
    bias = b_ref[...]

    def fill(it, carry):
        r0 = pl.multiple_of(it * tb, tb)
        pad_ref[pl.ds(halo + r0, tb), :] = ga_ref[pl.ds(r0, tb), :] * _sigmoid(gb_ref[pl.ds(r0, tb), :])
        return carry

    lax.fori_loop(0, n // tb, fill, 0)

    def body(it, carry):
        r0 = pl.multiple_of(it * tb, tb)
        acc = jnp.broadcast_to(bias, (tb, LANES))
        for k in range(CONV_K):
            acc = acc + w_ref[k:k + 1, :] * pad_ref[pl.ds(r0 + k * stride, tb), :]
        o_ref[pl.ds(r0, tb), :] = acc
        return carry

    lax.fori_loop(0, n // tb, body, 0)


def _glu_conv_kernel(ga_ref, gb_ref, w_ref, b_ref, o_ref, pad_rows, pad_cols, pad_ctx,
                     *, n, nc, n_row_tiles):
    c = pl.program_id(0)

    @pl.when(c < n_row_tiles)
    def _():
        _conv_runs(ga_ref, gb_ref, w_ref, b_ref, o_ref, pad_rows, row0=0, seq=GRID_W,
                   n_seq=n // GRID_W)

    @pl.when(c >= n_row_tiles)
    def _():
        _conv_stride(ga_ref, gb_ref, w_ref, b_ref, o_ref, pad_cols, n=n, stride=GRID_W)

    if nc:
        _conv_runs(ga_ref, gb_ref, w_ref, b_ref, o_ref, pad_ctx, row0=n, seq=nc, n_seq=1)


def _glu_conv_call(glu, conv_w, conv_b, *, n, nc, d_conv):
    rows = n + nc
    half = d_conv // 2
    assert half % LANES == 0 and n % (4 * GRID_W) == 0
    gblk = d_conv // LANES
    wpad = jnp.zeros((32, d_conv), F32).at[:CONV_K].set(conv_w)
    bias = conv_b.reshape(1, d_conv)
    kern = functools.partial(_glu_conv_kernel, n=n, nc=nc, n_row_tiles=half // LANES)
    return pl.pallas_call(
        kern,
        grid=(d_conv // LANES,),
        in_specs=[pl.BlockSpec((rows, LANES), lambda c: (0, c)),
                  pl.BlockSpec((rows, LANES), lambda c: (0, gblk + c)),
                  pl.BlockSpec((32, LANES), lambda c: (0, c)),
                  pl.BlockSpec((1, LANES), lambda c: (0, c))],
        out_specs=pl.BlockSpec((rows, LANES), lambda c: (0, c)),
        out_shape=jax.ShapeDtypeStruct((rows, d_conv), F32),
        scratch_shapes=[pltpu.VMEM((4, GRID_W + 2 * CONV_GAP, LANES), F32),
                        pltpu.VMEM((n + 2 * (CONV_K // 2) * GRID_W, LANES), F32),
                        pltpu.VMEM((1, max(nc, SUBLANES) + 2 * CONV_GAP, LANES), F32)],
        compiler_params=_cparams("parallel"),
        name="glu_conv",
    )(glu, glu, wpad, bias)


def _lnswish_kernel(*refs, n_in):
    ins = refs[:n_in]
    g_ref, b_ref, o_ref = refs[n_in:]
    x = jnp.concatenate([r[...] for r in ins], axis=1) if n_in > 1 else ins[0][...]
    mu = jnp.mean(x, axis=-1, keepdims=True)
    xc = x - mu
    var = jnp.mean(xc * xc, axis=-1, keepdims=True)
    y = xc * lax.rsqrt(var + LN_EPS) * g_ref[...] + b_ref[...]
    o_ref[...] = _silu(y).astype(o_ref.dtype)


def _lnswish_call(parts, ln_g, ln_b):
    rows = parts[0].shape[0]
    d = sum(p.shape[1] for p in parts)
    tm = ROW_TILE
    return pl.pallas_call(
        functools.partial(_lnswish_kernel, n_in=len(parts)),
        grid=(rows // tm,),
        in_specs=[*[pl.BlockSpec((tm, p.shape[1]), lambda i: (i, 0)) for p in parts],
                  pl.BlockSpec((1, d), lambda i: (0, 0)),
                  pl.BlockSpec((1, d), lambda i: (0, 0))],
        out_specs=pl.BlockSpec((tm, d), lambda i: (i, 0)),
        out_shape=jax.ShapeDtypeStruct((rows, d), BF16),
        compiler_params=_cparams("parallel"),
        name="conv_ln_swish",
    )(*parts, ln_g.reshape(1, d), ln_b.reshape(1, d))


def _resid_ln(x, t, gate, g, b, alpha):
    v = alpha * x + gate * t
    mu = jnp.mean(v, axis=-1, keepdims=True)
    vc = v - mu
    var = jnp.mean(vc * vc, axis=-1, keepdims=True)
    return vc * lax.rsqrt(var + LN_EPS) * g + b


def _resid_ln_router_kernel(x_ref, t_ref, mod_ref, g_ref, b_ref, wr_ref, xo_ref, aff_ref,
                            *, alpha):
    mod = mod_ref[...]
    xn = _resid_ln(x_ref[...], t_ref[...], mod[2:3, :], g_ref[...], b_ref[...], alpha)
    xo_ref[...] = xn
    u = xn * (1.0 + mod[4:5, :]) + mod[3:4, :]
    logits = jnp.dot(u, wr_ref[...], preferred_element_type=F32, precision=lax.Precision.HIGHEST)
    mx = jnp.max(logits, axis=-1, keepdims=True)
    ex = jnp.exp(logits - mx)
    aff_ref[...] = ex / jnp.sum(ex, axis=-1, keepdims=True)


def _resid_ln_kernel(x_ref, t_ref, mod_ref, g_ref, b_ref, xo_ref, *, alpha):
    mod = mod_ref[...]
    xo_ref[...] = _resid_ln(x_ref[...], t_ref[...], mod[5:6, :], g_ref[...], b_ref[...], alpha)


def _resid_call(x, t, mods, ln_g, ln_b, rows, n_lat_tiles, alpha, w_router=None):
    d = x.shape[1]
    tm = ROW_TILE
    row = pl.BlockSpec((tm, d), lambda i: (i, 0))
    vec = pl.BlockSpec((1, d), lambda i: (0, 0))
    mod_spec = pl.BlockSpec((None, N_MOD, d), lambda i: (jnp.where(i >= n_lat_tiles, 1, 0), 0, 0))
    if w_router is None:
        return pl.pallas_call(
            functools.partial(_resid_ln_kernel, alpha=alpha),
            grid=(rows // tm,),
            in_specs=[row, row, mod_spec, vec, vec],
            out_specs=row,
            out_shape=jax.ShapeDtypeStruct((rows, d), F32),
            compiler_params=_cparams("parallel"),
            name="resid_ln",
        )(x, t, mods, ln_g.reshape(1, d), ln_b.reshape(1, d))
    ne = w_router.shape[1]
    return pl.pallas_call(
        functools.partial(_resid_ln_router_kernel, alpha=alpha),
        grid=(rows // tm,),
        in_specs=[row, row, mod_spec, vec, vec, pl.BlockSpec((d, ne), lambda i: (0, 0))],
        out_specs=[row, pl.BlockSpec((tm, ne), lambda i: (i, 0))],
        out_shape=[jax.ShapeDtypeStruct((rows, d), F32),
                   jax.ShapeDtypeStruct((rows, ne), F32)],
        compiler_params=_cparams("parallel"),
        name="resid_ln_router",
    )(x, t, mods, ln_g.reshape(1, d), ln_b.reshape(1, d), w_router)


def _experts_kernel(idx_ref, x_hbm, mod_ref, wg_ref, wu_ref, wd_ref, g_ref, o_ref,
                    rows_ref, xe_ref, sem, *, rc, cap_lat):
    e = pl.program_id(0)
    f = pl.program_id(1)
    cap = xe_ref.shape[0]

    def start_gather(ee):
        def issue(r, carry):
            t = idx_ref[ee, r]
            pltpu.make_async_copy(x_hbm.at[pl.ds(t, 1), :], rows_ref.at[pl.ds(r, 1), :], sem).start()
            return carry
        lax.fori_loop(0, cap, issue, 0, unroll=8)

    @pl.when(jnp.logical_and(e == 0, f == 0))
    def _():
        start_gather(0)

    @pl.when(f == 0)
    def _():
        pltpu.make_async_copy(rows_ref, rows_ref, sem).wait()
        mod = mod_ref[...]
        for lo, hi, m in ((0, cap_lat, 0), (cap_lat, cap, 1)):
            if hi > lo:
                sh, sc = mod[m, 3:4, :], mod[m, 4:5, :]
                xe_ref[lo:hi, :] = (rows_ref[lo:hi, :] * (1.0 + sc) + sh).astype(BF16)
        o_ref[...] = jnp.zeros_like(o_ref)

        @pl.when(e + 1 < pl.num_programs(0))
        def _():
            start_gather(e + 1)

    wg = wg_ref[...].astype(BF16)
    wu = wu_ref[...].astype(BF16)
    wd = wd_ref[...].astype(BF16)
    for r0 in range(0, cap, rc):
        xe = xe_ref[r0:r0 + rc, :]
        hg = jnp.dot(xe, wg, preferred_element_type=F32)
        hu = jnp.dot(xe, wu, preferred_element_type=F32)
        h = (_silu(hg) * hu).astype(BF16)
        o_ref[r0:r0 + rc, :] += jnp.dot(h, wd, preferred_element_type=F32)

    @pl.when(f == pl.num_programs(1) - 1)
    def _():
        o_ref[...] = o_ref[...] * g_ref[...]


def _row_chunk(cap, limit=384, align=16):
    best = None
    for rc in range(align, min(cap, limit) + 1, align):
        if cap % rc == 0:
            best = rc
    assert best is not None, cap
    return best


def _experts_call(idx, xs, mods, w_gate, w_up, w_down, gsel, layer, cap_lat):
    ne, cap = idx.shape
    d = xs.shape[1]
    dexp = w_gate.shape[3]
    tf = 256 if dexp % 256 == 0 else dexp
    grid_spec = pltpu.PrefetchScalarGridSpec(
        num_scalar_prefetch=1,
        grid=(ne, dexp // tf),
        in_specs=[pl.BlockSpec(memory_space=pl.ANY),
                  pl.BlockSpec((2, N_MOD, d), lambda e, f, idx: (0, 0, 0)),
                  pl.BlockSpec((None, None, d, tf), lambda e, f, idx: (layer, e, 0, f)),
                  pl.BlockSpec((None, None, d, tf), lambda e, f, idx: (layer, e, 0, f)),
                  pl.BlockSpec((None, None, tf, d), lambda e, f, idx: (layer, e, f, 0)),
                  pl.BlockSpec((None, cap, 1), lambda e, f, idx: (e, 0, 0))],
        out_specs=pl.BlockSpec((None, cap, d), lambda e, f, idx: (e, 0, 0)),
        scratch_shapes=[pltpu.VMEM((cap, d), F32), pltpu.VMEM((cap, d), BF16),
                        pltpu.SemaphoreType.DMA(())],
    )
    return pl.pallas_call(
        functools.partial(_experts_kernel, rc=_row_chunk(cap), cap_lat=cap_lat),
        grid_spec=grid_spec,
        out_shape=jax.ShapeDtypeStruct((ne, cap, d), F32),
        compiler_params=_cparams("arbitrary", "arbitrary"),
        name="experts",
    )(idx, xs, mods, w_gate, w_up, w_down, gsel)


def _route(aff, n_tok):
    cap = max(1, EC_CAPACITY_FACTOR * n_tok // N_EXPERTS)
    g, idx = lax.top_k(aff.T, cap)
    return g, idx


COMBINE_CHUNK = 256


def _combine_kernel(src_ref, bnd_ref, ye_hbm, tok_ref, x_ref, mod_ref, g_ref, b_ref, xo_ref,
                    buf_ref, acc_ref, cnt_ref, sem, *, alpha, n_chunks):
    b = pl.program_id(0)
    tb = x_ref.shape[0]
    cs = COMBINE_CHUNK

    def issue(kk):
        slot = kk % 2
        base = kk * cs

        def one(i, carry):
            pltpu.make_async_copy(ye_hbm.at[pl.ds(src_ref[base + i], 1), :],
                                  buf_ref.at[slot, pl.ds(i, 1), :], sem.at[slot]).start()
            return carry
        lax.fori_loop(0, cs, one, 0, unroll=8)
        cnt_ref[0] = kk + 1

    @pl.when(b == 0)
    def _():
        cnt_ref[0] = 0
        cnt_ref[1] = 0
        issue(0)

    lo, hi = bnd_ref[b], bnd_ref[b + 1]
    k0 = lo // cs
    k1 = jnp.where(hi > lo, (hi + cs - 1) // cs, k0)
    acc_ref[...] = jnp.zeros_like(acc_ref)
    row_tok = b * tb + lax.broadcasted_iota(jnp.int32, (tb, cs), 0)

    def chunk(kk, carry):
        slot = kk % 2

        @pl.when(kk >= cnt_ref[1])
        def _():
            pltpu.make_async_copy(buf_ref.at[slot], buf_ref.at[slot], sem.at[slot]).wait()
            cnt_ref[1] = kk + 1

        @pl.when(jnp.logical_and(kk + 1 < n_chunks, kk + 1 >= cnt_ref[0]))
        def _():
            issue(kk + 1)

        toks = tok_ref[:, pl.ds(pl.multiple_of(kk * cs, cs), cs)]
        onehot = jnp.where(row_tok == toks, 1.0, 0.0).astype(BF16)
        acc_ref[...] += jnp.dot(onehot, buf_ref[slot].astype(BF16), preferred_element_type=F32)
        return carry

    lax.fori_loop(k0, k1, chunk, 0)

    mod = mod_ref[...]
    xo_ref[...] = _resid_ln(x_ref[...], acc_ref[...], mod[5:6, :], g_ref[...], b_ref[...], alpha)


def _combine_call(ye, idx, xs1, mods, ln_g, ln_b, rows, n_lat_tiles, alpha):
    ne, cap, d = ye.shape
    tb = ROW_TILE
    cs = COMBINE_CHUNK
    nb = rows // tb
    n_list = ne * cap
    n_pad = -(-n_list // cs) * cs
    tok_flat = idx.reshape(-1)
    order = jnp.argsort(tok_flat).astype(jnp.int32)
    tok_sorted = jnp.take(tok_flat, order)
    bounds = jnp.searchsorted(tok_sorted, jnp.arange(nb + 1, dtype=jnp.int32) * tb).astype(jnp.int32)
    tok_pad = jnp.full((1, n_pad), -1, jnp.int32).at[0, :n_list].set(tok_sorted)
    src_pad = jnp.zeros((n_pad,), jnp.int32).at[:n_list].set(order)
    row = pl.BlockSpec((tb, d), lambda i, *_: (i, 0))
    vec = pl.BlockSpec((1, d), lambda i, *_: (0, 0))
    grid_spec = pltpu.PrefetchScalarGridSpec(
        num_scalar_prefetch=2,
        grid=(nb,),
        in_specs=[pl.BlockSpec(memory_space=pl.ANY),
                  pl.BlockSpec((1, n_pad), lambda i, *_: (0, 0)),
                  row,
                  pl.BlockSpec((None, N_MOD, d),
                               lambda i, *_: (jnp.where(i >= n_lat_tiles, 1, 0), 0, 0)),
                  vec, vec],
        out_specs=row,
        scratch_shapes=[pltpu.VMEM((2, cs, d), F32), pltpu.VMEM((tb, d), F32),
                        pltpu.SMEM((2,), jnp.int32), pltpu.SemaphoreType.DMA((2,))],
    )
    return pl.pallas_call(
        functools.partial(_combine_kernel, alpha=alpha, n_chunks=n_pad // cs),
        grid_spec=grid_spec,
        out_shape=jax.ShapeDtypeStruct((rows, d), F32),
        compiler_params=_cparams("arbitrary"),
        name="moe_combine_ln",
    )(src_pad, bounds, ye.reshape(n_list, d), tok_pad, xs1, mods,
      ln_g.reshape(1, d), ln_b.reshape(1, d))


def _spread_matrix(n_heads, direction):
    r = jnp.arange(4 * n_heads)[:, None] % (2 * n_heads)
    ch = jnp.arange(n_heads * HEAD_DIM)[None, :] // HEAD_DIM
    return (r == ch + direction * n_heads).astype(BF16)


def kernel(x, c, ctx, c_ctx, w_ada, b_ada, w_in, ssm_conv_w, ssm_conv_b, ssm_dt_bias, ssm_a_log,
           ssm_d, ssm_norm_w, w_ssm_out, conv_dw_w, conv_dw_b, conv_ln_g, conv_ln_b, w_conv_out,
           w_o, ln1_g, ln1_b, w_router, w_exp_gate, w_exp_up, w_exp_down, ln2_g, ln2_b):
    batch, n, d = x.shape
    nc = ctx.shape[1]
    depth = w_ada.shape[0]
    assert batch == 1 and n % ROW_TILE == 0 and nc % ROW_TILE == 0 and n % nc == 0
    d_inner = 2 * d
    n_heads = d_inner // HEAD_DIM
    d_bc = GROUPS * STATE
    d_xbc = d_inner + 2 * d_bc
    d_conv = d
    o_xbc = d_inner
    o_dt = o_xbc + d_xbc
    o_glu = o_dt + 2 * n_heads
    alpha = (2 * depth) ** 0.25
    nl, ncc = n // CHUNK, nc // CHUNK
    n_lat_tiles = n // ROW_TILE

    xs = jnp.concatenate([x[0], ctx[0]], axis=0)
    cvec = jnp.zeros((SUBLANES, d), F32).at[0].set(c[0]).at[1].set(c_ctx)
    mods_all = _ada_call(cvec, w_ada, b_ada)
    e_mats = [_spread_matrix(n_heads, dr) for dr in range(2)]

    for i in range(depth):
        last = i == depth - 1
        mods = mods_all[i, :2].reshape(2, N_MOD, d)
        rows_all = n + nc
        rows = n if last else rows_all

        u_in = _modulate_call(xs, mods, rows_all, n_lat_tiles)
        z = _mm_call(u_in, w_in, i, 0, d_inner, rows=rows, name="proj_z")
        xbc = _mm_call(u_in, w_in, i, o_xbc, d_xbc, rows=rows_all, name="proj_xbc",
                       slab_major=True)
        dtraw = _mm_call(u_in, w_in, i, o_dt, 2 * n_heads, rows=rows_all, name="proj_dt")
        glu = _mm_call(u_in, w_in, i, o_glu, 2 * d_conv + 2 * d, rows=rows, name="proj_glu_gate")

        conv_w = jnp.zeros((SUBLANES, d_xbc), F32).at[:SSM_CONV].set(ssm_conv_w[i])
        conv_b = ssm_conv_b[i].reshape(1, d_xbc)
        dt_bias = ssm_dt_bias[i].reshape(1, 2 * n_heads)
        a_row = -jnp.exp(ssm_a_log[i].astype(F32)).reshape(1, 2 * n_heads)
        d_row = jnp.repeat(ssm_d[i], HEAD_DIM).reshape(1, d_inner)
        y = _ssd_call(xbc, dtraw, conv_w, conv_b, dt_bias, a_row, e_mats, d_row,
                      nl=nl, ncc=ncc, d_inner=d_inner)
        a_act = _gnorm_call(y, z, ssm_norm_w[i].reshape(1, d_inner), rows, d_inner)

        hcv = _glu_conv_call(glu, conv_dw_w[i], conv_dw_b[i], n=n, nc=rows - n, d_conv=d_conv)
        b_act = _lnswish_call([hcv], conv_ln_g[i], conv_ln_b[i])

        t1 = _mm_call(a_act, w_ssm_out, i, 0, d, rows=rows, name="ssm_out",
                      epi_fn=lambda acc, gt: _sigmoid(gt) * acc,
                      epi_cols=[2 * d_conv], epi_args=[glu])
        t2 = _mm_call(b_act, w_conv_out, i, 0, d, rows=rows, name="conv_out", out_dtype=BF16,
                      epi_fn=lambda acc, gt, prev: prev + _sigmoid(gt) * acc,
                      epi_cols=[2 * d_conv + d, 0], epi_args=[glu, t1])
        mix = _mm_call(t2, w_o, i, 0, d, rows=rows, name="w_o")

        xs1, aff = _resid_call(xs, mix, mods, ln1_g[i], ln1_b[i], rows, n_lat_tiles, alpha,
                               w_router=w_router[i])

        g_l, idx_l = _route(aff[:n], n)
        gsel, idx = g_l, idx_l
        if not last:
            g_c, idx_c = _route(aff[n:], nc)
            gsel = jnp.concatenate([g_l, g_c], axis=1)
            idx = jnp.concatenate([idx_l, idx_c + n], axis=1)
        ye = _experts_call(idx, xs1, mods, w_exp_gate, w_exp_up, w_exp_down, gsel[..., None], i,
                           idx_l.shape[1])
        xs = _combine_call(ye, idx, xs1, mods, ln2_g[i], ln2_b[i], rows, n_lat_tiles, alpha)

    return xs[:n].reshape(1, n, d)
```

```python
import functools

import jax
import jax.numpy as jnp
from jax import lax
from jax.experimental import pallas as pl
from jax.experimental.pallas import tpu as pltpu

F32 = jnp.float32
BF16 = jnp.bfloat16

HEAD_DIM = 64
GROUPS = 8
STATE = 128
SSM_CONV = 5
CHUNK = 128
CONV_K = 31
GRID_W = 64
N_EXPERTS = 16
EC_CAPACITY_FACTOR = 2
N_MOD = 6
LN_EPS = 1e-5

LANES = 128
SUBLANES = 8
VMEM_LIMIT = 60 * 1024 * 1024
MM_VMEM_BUDGET = 47 * 1024 * 1024
MXU_FLOPS = 1.1e15
MXU_WEIGHT_ROWS = 256
HBM_BYTES_PER_S = 3.3e12
GRID_STEP_S = 0.35e-6

ROW_TILE = 256
HALO = SUBLANES


def _cparams(*sem):
    return pltpu.CompilerParams(dimension_semantics=sem, vmem_limit_bytes=VMEM_LIMIT)


def _sigmoid(v):
    return jax.nn.sigmoid(v)


def _silu(v):
    return v * jax.nn.sigmoid(v)


def _softplus(v):
    return jnp.maximum(v, 0.0) + jnp.log1p(jnp.exp(-jnp.abs(v)))


def _ada_kernel(c_ref, w_ref, b_ref, o_ref):
    a = _silu(c_ref[...])
    o_ref[...] = jnp.dot(a, w_ref[...], preferred_element_type=F32,
                         precision=lax.Precision.HIGHEST) + b_ref[...]


def _ada_call(cvec, w_ada, b_ada):
    depth, d, nm = w_ada.shape
    tn = 1024 if nm % 1024 == 0 else nm
    return pl.pallas_call(
        _ada_kernel,
        grid=(depth, nm // tn),
        in_specs=[
            pl.BlockSpec((SUBLANES, d), lambda l, j: (0, 0)),
            pl.BlockSpec((None, d, tn), lambda l, j: (l, 0, j)),
            pl.BlockSpec((None, 1, tn), lambda l, j: (l, 0, j)),
        ],
        out_specs=pl.BlockSpec((None, SUBLANES, tn), lambda l, j: (l, 0, j)),
        out_shape=jax.ShapeDtypeStruct((depth, SUBLANES, nm), F32),
        compiler_params=_cparams("arbitrary", "arbitrary"),
        name="adaln",
    )(cvec, w_ada, b_ada.reshape(depth, 1, nm))


def _mm_kernel(*refs, n_epi, epi_fn, tm):
    x_ref, w_ref = refs[0], refs[1]
    epi = refs[2:2 + n_epi]
    o_ref = refs[2 + n_epi]
    wb_ref = refs[3 + n_epi]
    i = pl.program_id(1)

    @pl.when(i == 0)
    def _():
        wb_ref[...] = w_ref[...].astype(BF16)

    if x_ref.shape[0] == tm:
        x = x_ref[...]
    else:
        x = x_ref[pl.ds(pl.multiple_of(i * tm, tm), tm), :]
    acc = jnp.dot(x, wb_ref[...], preferred_element_type=F32)
    if epi_fn is not None:
        acc = epi_fn(acc, *[r[...] for r in epi])
    if len(o_ref.shape) == 3:
        for jj in range(o_ref.shape[0]):
            o_ref[jj] = acc[:, jj * LANES:(jj + 1) * LANES].astype(o_ref.dtype)
    else:
        o_ref[...] = acc.astype(o_ref.dtype)


def _mm_tiles(rows, k, n_out, n_epi, out_bytes):
    best = None
    for tm in range(LANES, rows + 1, LANES):
        if rows % tm:
            continue
        for tn in (1024, 512, 256, 128):
            if n_out % tn:
                continue
            for resident in (False, True):
                x_vmem = rows * k * 2 if resident else 2 * tm * k * 2
                vmem = (x_vmem + 2 * k * tn * 4 + k * tn * 2 + 2 * tm * tn * out_bytes
                        + n_epi * 3 * tm * tn * 4 + tm * tn * 4)
                if vmem > MM_VMEM_BUDGET:
                    continue
                x_reads = 1 if resident else n_out // tn
                t_mxu = 2.0 * rows * k * n_out / MXU_FLOPS * (1.0 + MXU_WEIGHT_ROWS / tm)
                if tn < MXU_WEIGHT_ROWS:
                    t_mxu *= 2.0
                t_hbm = (rows * k * 2.0 * x_reads + k * n_out * 4.0
                         + rows * n_out * (out_bytes + 4.0 * n_epi)) / HBM_BYTES_PER_S
                t = t_mxu + 0.5 * t_hbm + (rows // tm) * (n_out // tn) * GRID_STEP_S
                if best is None or t < best[0]:
                    best = (t, tm, tn, resident)
    assert best is not None
    return best[1:]


def _mm_call(x, w, layer, col_off, n_out, *, rows, name, out_dtype=F32,
             epi_fn=None, epi_cols=(), epi_args=(), slab_major=False):
    k = x.shape[1]
    tm, tn, resident = _mm_tiles(rows, k, n_out, len(epi_args), jnp.dtype(out_dtype).itemsize)
    assert col_off % LANES == 0
    kern = functools.partial(_mm_kernel, n_epi=len(epi_args), epi_fn=epi_fn, tm=tm)
    if resident and rows > tm:
        x_spec = pl.BlockSpec((rows, k), lambda j, i: (0, 0), pipeline_mode=pl.Buffered(1))
    else:
        x_spec = pl.BlockSpec((tm, k), lambda j, i: (i, 0))
    epi_specs = [pl.BlockSpec((tm, tn), functools.partial(
        lambda j, i, off: (i, j + off), off=c // tn)) for c in epi_cols]
    assert all(c % tn == 0 for c in epi_cols)
    if slab_major:
        out_spec = pl.BlockSpec((tn // LANES, tm, LANES), lambda j, i: (j, i, 0))
        out_shape = jax.ShapeDtypeStruct((n_out // LANES, rows, LANES), out_dtype)
    else:
        out_spec = pl.BlockSpec((tm, tn), lambda j, i: (i, j))
        out_shape = jax.ShapeDtypeStruct((rows, n_out), out_dtype)
    return pl.pallas_call(
        kern,
        grid=(n_out // tn, rows // tm),
        in_specs=[x_spec,
                  pl.BlockSpec((pl.Element(k), pl.Element(tn)),
                               lambda j, i: (layer * k, pl.multiple_of(col_off + j * tn, LANES))),
                  *epi_specs],
        out_specs=out_spec,
        out_shape=out_shape,
        scratch_shapes=[pltpu.VMEM((k, tn), BF16)],
        compiler_params=_cparams("arbitrary", "arbitrary"),
        name=name,
    )(x, w.reshape(-1, w.shape[-1]), *epi_args)


def _modulate_kernel(x_ref, mod_ref, o_ref):
    mod = mod_ref[...]
    o_ref[...] = (x_ref[...] * (1.0 + mod[1:2, :]) + mod[0:1, :]).astype(o_ref.dtype)


def _modulate_call(xs, mods, rows, n_lat_tiles):
    d = xs.shape[1]
    tm = ROW_TILE
    return pl.pallas_call(
        _modulate_kernel,
        grid=(rows // tm,),
        in_specs=[pl.BlockSpec((tm, d), lambda i: (i, 0)),
                  pl.BlockSpec((None, N_MOD, d),
                               lambda i: (jnp.where(i >= n_lat_tiles, 1, 0), 0, 0))],
        out_specs=pl.BlockSpec((tm, d), lambda i: (i, 0)),
        out_shape=jax.ShapeDtypeStruct((rows, d), BF16),
        compiler_params=_cparams("parallel"),
        name="modulate",
    )(xs, mods)


def _ssd_chunk_of_step(s, direction, nl, ncc):
    if direction == 0:
        return jnp.where(s < ncc, nl + s, s - ncc)
    return jnp.where(s < ncc, nl + ncc - 1 - s, nl - 1 - (s - ncc))


def _ssd_kernel(*refs, direction, nl, ncc, d_inner):
    if direction == 0:
        (xm_ref, xp_ref, xn_ref, dt_ref, cw_ref, cb_ref, dtb_ref, arow_ref, e_ref, dsk_ref,
         y_ref, xc_ref, state_ref, ext_ref) = refs
        yprev_ref = None
    else:
        (xc_ref, dt_ref, dtb_ref, arow_ref, e_ref, yprev_ref, y_ref, state_ref, ext_ref,
         ys_ref) = refs
    q = CHUNK
    d_bc = GROUPS * STATE
    d_xbc = d_inner + 2 * d_bc
    gw = d_inner // GROUPS
    hpg = gw // HEAD_DIM
    n_heads = d_inner // HEAD_DIM
    n_grp = q // SUBLANES

    s = pl.program_id(0)
    c = _ssd_chunk_of_step(s, direction, nl, ncc)

    @pl.when(s == 0)
    def _():
        state_ref[...] = jnp.zeros_like(state_ref)

    def slabs(ref3, rows):
        return jnp.concatenate([ref3[j, rows, :] for j in range(ref3.shape[0])], axis=1)

    if direction == 0:
        pad = SSM_CONV // 2
        seg_first = jnp.logical_or(c == 0, c == nl)
        seg_last = jnp.logical_or(c == nl - 1, c == nl + ncc - 1)
        every = slice(None)
        prev = jnp.where(seg_first, 0.0, slabs(xp_ref, every))
        nxt = jnp.where(seg_last, 0.0, slabs(xn_ref, every))
        sub = lax.broadcasted_iota(jnp.int32, (SUBLANES, d_xbc), 0)
        for m in range(n_grp):
            grp = slabs(xm_ref, pl.ds(m, SUBLANES, stride=n_grp))
            ext_ref[(pad + m) * SUBLANES:(pad + m + 1) * SUBLANES, :] = grp
            if m >= n_grp - pad:
                j = m - (n_grp - pad)
                halo = prev[HALO - pad + j:HALO - pad + j + 1, :]
                ext_ref[j * SUBLANES:(j + 1) * SUBLANES, :] = jnp.where(
                    sub == 0, halo, pltpu.roll(grp, 1, axis=0))
            if m < pad:
                halo = nxt[m:m + 1, :]
                ext_ref[(pad + n_grp + m) * SUBLANES:(pad + n_grp + m + 1) * SUBLANES, :] = (
                    jnp.where(sub == SUBLANES - 1, halo, pltpu.roll(grp, SUBLANES - 1, axis=0)))
        ct = 512
        for j in range(d_xbc // ct):
            cs = slice(j * ct, (j + 1) * ct)
            acc = jnp.broadcast_to(cb_ref[:, cs], (q, ct))
            for k in range(SSM_CONV):
                acc = acc + cw_ref[k:k + 1, cs] * ext_ref[k * SUBLANES:k * SUBLANES + q, cs]
            xc_ref[:, cs] = _silu(acc)

    dt_raw = jnp.concatenate([dt_ref[pl.ds(m, SUBLANES, stride=n_grp), :] for m in range(n_grp)],
                             axis=0)
    dt = _softplus(dt_raw + dtb_ref[...])
    a = dt * arow_ref[...]
    ri = lax.broadcasted_iota(jnp.int32, (q, q), 0)
    ci = lax.broadcasted_iota(jnp.int32, (q, q), 1)
    tok_i = (ri % SUBLANES) * n_grp + ri // SUBLANES
    tok_j = (ci % SUBLANES) * n_grp + ci // SUBLANES
    causal = (tok_i >= tok_j) if direction == 0 else (tok_i <= tok_j)
    cum = jnp.dot(causal.astype(F32), a, preferred_element_type=F32,
                  precision=lax.Precision.HIGHEST)
    tot = cum[q - 1:q, :] if direction == 0 else cum[0:1, :]
    e_in = jnp.exp(cum)
    w_st = dt * jnp.exp(tot - cum)
    cum_t = cum.T
    dt_t = dt.T

    stack = jnp.concatenate([w_st, e_in], axis=0)
    hi = stack.astype(BF16)
    lo = (stack - hi.astype(F32)).astype(BF16)
    spread = jnp.dot(jnp.concatenate([hi, lo], axis=1), e_ref[...],
                     preferred_element_type=F32)
    ext_ref[0:2 * q, 0:d_inner] = spread
    dec_row = q - 1 if direction == 0 else 0

    lane = lax.broadcasted_iota(jnp.int32, (q, LANES), 1)
    first_head = lane < HEAD_DIM
    hoff = direction * n_heads

    for g in range(GROUPS):
        bg = xc_ref[:, d_inner + g * STATE:d_inner + (g + 1) * STATE]
        cg = xc_ref[:, d_inner + d_bc + g * STATE:d_inner + d_bc + (g + 1) * STATE]
        bgb = bg.astype(BF16)
        cgb = cg.astype(BF16)
        scores = lax.dot_general(cgb, bgb, (((1,), (1,)), ((), ())),
                                 preferred_element_type=F32)
        gs = slice(g * gw, (g + 1) * gw)
        st = state_ref[:, gs]
        y_g = jnp.dot(cgb, st.astype(BF16), preferred_element_type=F32) * ext_ref[q:2 * q, gs]
        xg = xc_ref[:, gs]
        xw = (xg * ext_ref[0:q, gs]).astype(BF16)
        s_new = jnp.dot(bg.T.astype(BF16), xw, preferred_element_type=F32)
        state_ref[:, gs] = st * ext_ref[q + dec_row:q + dec_row + 1, gs] + s_new
        parts = []
        for p in range(hpg // 2):
            slab = xg[:, p * LANES:(p + 1) * LANES]
            y_p = None
            for half in range(2):
                h = hoff + g * hpg + 2 * p + half
                seg = cum[:, h:h + 1] - cum_t[h:h + 1, :]
                m = scores * jnp.exp(jnp.where(causal, seg, -jnp.inf)) * dt_t[h:h + 1, :]
                keep = first_head if half == 0 else jnp.logical_not(first_head)
                xh = jnp.where(keep, slab, 0.0).astype(BF16)
                t = jnp.dot(m.astype(BF16), xh, preferred_element_type=F32)
                y_p = t if y_p is None else y_p + t
            parts.append(y_p)
        y_g = y_g + jnp.concatenate(parts, axis=1)
        if direction == 0:
            y_ref[:, gs] = y_g + dsk_ref[:, gs] * xg
        else:
            y_g = y_g + yprev_ref[:, gs]
            for jl in range(gw // LANES):
                ys_ref[...] = y_g[:, jl * LANES:(jl + 1) * LANES]
                jg = g * (gw // LANES) + jl
                for a in range(n_grp):
                    start = (a % 2) * (q // 2) + a // 2
                    y_ref[jg, a * SUBLANES:(a + 1) * SUBLANES, :] = (
                        ys_ref[pl.ds(start, SUBLANES, stride=SUBLANES), :])


def _ssd_call(xbc, dtraw, conv_w, conv_b, dt_bias, a_row, e_mats, d_row, *, nl, ncc, d_inner):
    rows = (nl + ncc) * CHUNK
    d_xbc = d_inner + 2 * GROUPS * STATE
    nh2 = dtraw.shape[1]
    q = CHUNK
    qh = q // HALO
    nchunks = nl + ncc
    full = lambda shape: pl.BlockSpec(shape, lambda s: (0,) * len(shape))
    scratch = [pltpu.VMEM((STATE, d_inner), F32), pltpu.VMEM((2 * q, d_xbc), F32)]

    def chunk_spec(width, direction):
        cmap = functools.partial(_ssd_chunk_of_step, direction=direction, nl=nl, ncc=ncc)
        return pl.BlockSpec((q, width), lambda s: (cmap(s), 0))

    cmap = functools.partial(_ssd_chunk_of_step, direction=0, nl=nl, ncc=ncc)
    cmap_b = functools.partial(_ssd_chunk_of_step, direction=1, nl=nl, ncc=ncc)
    ns_x, ns_y = d_xbc // LANES, d_inner // LANES
    main_spec = pl.BlockSpec((ns_x, q, LANES), lambda s: (0, cmap(s), 0))
    prev_spec = pl.BlockSpec((ns_x, HALO, LANES),
                             lambda s: (0, jnp.maximum(cmap(s) * qh - 1, 0), 0))
    next_spec = pl.BlockSpec((ns_x, HALO, LANES),
                             lambda s: (0, jnp.minimum((cmap(s) + 1) * qh, nchunks * qh - 1), 0))
    y_fwd, xc = pl.pallas_call(
        functools.partial(_ssd_kernel, direction=0, nl=nl, ncc=ncc, d_inner=d_inner),
        grid=(nchunks,),
        in_specs=[main_spec, prev_spec, next_spec, chunk_spec(nh2, 0),
                  full((SUBLANES, d_xbc)), full((1, d_xbc)), full((1, nh2)), full((1, nh2)),
                  full(e_mats[0].shape), full((1, d_inner))],
        out_specs=[chunk_spec(d_inner, 0), chunk_spec(d_xbc, 0)],
        out_shape=[jax.ShapeDtypeStruct((rows, d_inner), F32),
                   jax.ShapeDtypeStruct((rows, d_xbc), F32)],
        scratch_shapes=scratch,
        compiler_params=_cparams("arbitrary"),
        name="ssd_fwd",
    )(xbc, xbc, xbc, dtraw, conv_w, conv_b, dt_bias, a_row, e_mats[0], d_row)
    return pl.pallas_call(
        functools.partial(_ssd_kernel, direction=1, nl=nl, ncc=ncc, d_inner=d_inner),
        grid=(nchunks,),
        in_specs=[chunk_spec(d_xbc, 1), chunk_spec(nh2, 1), full((1, nh2)), full((1, nh2)),
                  full(e_mats[1].shape), chunk_spec(d_inner, 1)],
        out_specs=pl.BlockSpec((ns_y, q, LANES), lambda s: (0, cmap_b(s), 0)),
        out_shape=jax.ShapeDtypeStruct((ns_y, rows, LANES), F32),
        scratch_shapes=scratch + [pltpu.VMEM((q, LANES), F32)],
        compiler_params=_cparams("arbitrary"),
        name="ssd_bwd",
    )(xc, dtraw, dt_bias, a_row, e_mats[1], y_fwd)


def _gnorm_kernel(y_ref, z_ref, w_ref, o_ref, *, d_inner):
    gw = d_inner // GROUPS
    spg = gw // LANES
    for g in range(GROUPS):
        gs = slice(g * gw, (g + 1) * gw)
        y = jnp.concatenate([y_ref[g * spg + j] for j in range(spg)], axis=1)
        h = y * _silu(z_ref[:, gs])
        ms = jnp.mean(h * h, axis=-1, keepdims=True)
        o_ref[:, gs] = (h * lax.rsqrt(ms + LN_EPS) * w_ref[:, gs]).astype(o_ref.dtype)


def _gnorm_call(y, zx, norm_w, rows, d_inner):
    tm = ROW_TILE
    return pl.pallas_call(
        functools.partial(_gnorm_kernel, d_inner=d_inner),
        grid=(rows // tm,),
        in_specs=[pl.BlockSpec((d_inner // LANES, tm, LANES), lambda i: (0, i, 0)),
                  pl.BlockSpec((tm, d_inner), lambda i: (i, 0)),
                  pl.BlockSpec((1, d_inner), lambda i: (0, 0))],
        out_specs=pl.BlockSpec((tm, d_inner), lambda i: (i, 0)),
        out_shape=jax.ShapeDtypeStruct((rows, d_inner), BF16),
        compiler_params=_cparams("parallel"),
        name="gated_rmsnorm",
    )(y, zx, norm_w)


CONV_GAP = 2 * SUBLANES


def _conv_runs(ga_ref, gb_ref, w_ref, b_ref, o_ref, pad_ref, *, row0, seq, n_seq):
    half = CONV_K // 2
    gap = CONV_GAP
    rb = pad_ref.shape[0]
    pad_ref[:, 0:gap, :] = jnp.zeros((rb, gap, LANES), F32)
    pad_ref[:, gap + seq:gap + seq + gap, :] = jnp.zeros((rb, gap, LANES), F32)
    bias = b_ref[...]

    def body(it, carry):
        r0 = pl.multiple_of(row0 + it * (rb * seq), SUBLANES)
        ga = ga_ref[pl.ds(r0, rb * seq), :]
        gb = gb_ref[pl.ds(r0, rb * seq), :]
        pad_ref[:, gap:gap + seq, :] = (ga * _sigmoid(gb)).reshape(rb, seq, LANES)
        acc = jnp.broadcast_to(bias.reshape(1, 1, LANES), (rb, seq, LANES))
        for k in range(CONV_K):
            o = gap - half + k
            acc = acc + w_ref[k:k + 1, :].reshape(1, 1, LANES) * pad_ref[:, o:o + seq, :]
        o_ref[pl.ds(r0, rb * seq), :] = acc.reshape(rb * seq, LANES)
        return carry

    lax.fori_loop(0, n_seq // rb, body, 0)


def _conv_stride(ga_ref, gb_ref, w_ref, b_ref, o_ref, pad_ref, *, n, stride):
    half = CONV_K // 2
    halo = half * stride
    pad_ref[0:halo, :] = jnp.zeros((halo, LANES), F32)
    pad_ref[halo + n:halo + n + halo, :] = jnp.zeros((halo, LANES), F32)
    tb = 256
    bias = b_ref[...]

    def fill(it, carry):
        r0 = pl.multiple_of(it * tb, tb)
        pad_ref[pl.ds(halo + r0, tb), :] = ga_ref[pl.ds(r0, tb), :] * _sigmoid(gb_ref[pl.ds(r0, tb), :])
        return carry

    lax.fori_loop(0, n // tb, fill, 0)

    def body(it, carry):
        r0 = pl.multiple_of(it * tb, tb)
        acc = jnp.broadcast_to(bias, (tb, LANES))
        for k in range(CONV_K):
            acc = acc + w_ref[k:k + 1, :] * pad_ref[pl.ds(r0 + k * stride, tb), :]
        o_ref[pl.ds(r0, tb), :] = acc
        return carry

    lax.fori_loop(0, n // tb, body, 0)


def _glu_conv_kernel(ga_ref, gb_ref, w_ref, b_ref, o_ref, pad_rows, pad_cols, pad_ctx,
                     *, n, nc, n_row_tiles):
    c = pl.program_id(0)

    @pl.when(c < n_row_tiles)
    def _():
        _conv_runs(ga_ref, gb_ref, w_ref, b_ref, o_ref, pad_rows, row0=0, seq=GRID_W,
                   n_seq=n // GRID_W)

    @pl.when(c >= n_row_tiles)
    def _():
        _conv_stride(ga_ref, gb_ref, w_ref, b_ref, o_ref, pad_cols, n=n, stride=GRID_W)

    if nc:
        _conv_runs(ga_ref, gb_ref, w_ref, b_ref, o_ref, pad_ctx, row0=n, seq=nc, n_seq=1)


def _glu_conv_call(glu, conv_w, conv_b, *, n, nc, d_conv):
    rows = n + nc
    half = d_conv // 2
    assert half % LANES == 0 and n % (4 * GRID_W) == 0
    gblk = d_conv // LANES
    wpad = jnp.zeros((32, d_conv), F32).at[:CONV_K].set(conv_w)
    bias = conv_b.reshape(1, d_conv)
    kern = functools.partial(_glu_conv_kernel, n=n, nc=nc, n_row_tiles=half // LANES)
    return pl.pallas_call(
        kern,
        grid=(d_conv // LANES,),
        in_specs=[pl.BlockSpec((rows, LANES), lambda c: (0, c)),
                  pl.BlockSpec((rows, LANES), lambda c: (0, gblk + c)),
                  pl.BlockSpec((32, LANES), lambda c: (0, c)),
                  pl.BlockSpec((1, LANES), lambda c: (0, c))],
        out_specs=pl.BlockSpec((rows, LANES), lambda c: (0, c)),
        out_shape=jax.ShapeDtypeStruct((rows, d_conv), F32),
        scratch_shapes=[pltpu.VMEM((4, GRID_W + 2 * CONV_GAP, LANES), F32),
                        pltpu.VMEM((n + 2 * (CONV_K // 2) * GRID_W, LANES), F32),
                        pltpu.VMEM((1, max(nc, SUBLANES) + 2 * CONV_GAP, LANES), F32)],
        compiler_params=_cparams("parallel"),
        name="glu_conv",
    )(glu, glu, wpad, bias)


def _lnswish_kernel(*refs, n_in):
    ins = refs[:n_in]
    g_ref, b_ref, o_ref = refs[n_in:]
    x = jnp.concatenate([r[...] for r in ins], axis=1) if n_in > 1 else ins[0][...]
    mu = jnp.mean(x, axis=-1, keepdims=True)
    xc = x - mu
    var = jnp.mean(xc * xc, axis=-1, keepdims=True)
    y = xc * lax.rsqrt(var + LN_EPS) * g_ref[...] + b_ref[...]
    o_ref[...] = _silu(y).astype(o_ref.dtype)


def _lnswish_call(parts, ln_g, ln_b):
    rows = parts[0].shape[0]
    d = sum(p.shape[1] for p in parts)
    tm = ROW_TILE
    return pl.pallas_call(
        functools.partial(_lnswish_kernel, n_in=len(parts)),
        grid=(rows // tm,),
        in_specs=[*[pl.BlockSpec((tm, p.shape[1]), lambda i: (i, 0)) for p in parts],
                  pl.BlockSpec((1, d), lambda i: (0, 0)),
                  pl.BlockSpec((1, d), lambda i: (0, 0))],
        out_specs=pl.BlockSpec((tm, d), lambda i: (i, 0)),
        out_shape=jax.ShapeDtypeStruct((rows, d), BF16),
        compiler_params=_cparams("parallel"),
        name="conv_ln_swish",
    )(*parts, ln_g.reshape(1, d), ln_b.reshape(1, d))


def _resid_ln(x, t, gate, g, b, alpha):
    v = alpha * x + gate * t
    mu = jnp.mean(v, axis=-1, keepdims=True)
    vc = v - mu
    var = jnp.mean(vc * vc, axis=-1, keepdims=True)
    return vc * lax.rsqrt(var + LN_EPS) * g + b


def _resid_ln_router_kernel(x_ref, t_ref, mod_ref, g_ref, b_ref, wr_ref, xo_ref, aff_ref,
                            *, alpha):
    mod = mod_ref[...]
    xn = _resid_ln(x_ref[...], t_ref[...], mod[2:3, :], g_ref[...], b_ref[...], alpha)
    xo_ref[...] = xn
    u = xn * (1.0 + mod[4:5, :]) + mod[3:4, :]
    logits = jnp.dot(u, wr_ref[...], preferred_element_type=F32, precision=lax.Precision.HIGHEST)
    mx = jnp.max(logits, axis=-1, keepdims=True)
    ex = jnp.exp(logits - mx)
    aff_ref[...] = ex / jnp.sum(ex, axis=-1, keepdims=True)


def _resid_ln_kernel(x_ref, t_ref, mod_ref, g_ref, b_ref, xo_ref, *, alpha):
    mod = mod_ref[...]
    xo_ref[...] = _resid_ln(x_ref[...], t_ref[...], mod[5:6, :], g_ref[...], b_ref[...], alpha)


def _resid_call(x, t, mods, ln_g, ln_b, rows, n_lat_tiles, alpha, w_router=None):
    d = x.shape[1]
    tm = ROW_TILE
    row = pl.BlockSpec((tm, d), lambda i: (i, 0))
    vec = pl.BlockSpec((1, d), lambda i: (0, 0))
    mod_spec = pl.BlockSpec((None, N_MOD, d), lambda i: (jnp.where(i >= n_lat_tiles, 1, 0), 0, 0))
    if w_router is None:
        return pl.pallas_call(
            functools.partial(_resid_ln_kernel, alpha=alpha),
            grid=(rows // tm,),
            in_specs=[row, row, mod_spec, vec, vec],
            out_specs=row,
            out_shape=jax.ShapeDtypeStruct((rows, d), F32),
            compiler_params=_cparams("parallel"),
            name="resid_ln",
        )(x, t, mods, ln_g.reshape(1, d), ln_b.reshape(1, d))
    ne = w_router.shape[1]
    return pl.pallas_call(
        functools.partial(_resid_ln_router_kernel, alpha=alpha),
        grid=(rows // tm,),
        in_specs=[row, row, mod_spec, vec, vec, pl.BlockSpec((d, ne), lambda i: (0, 0))],
        out_specs=[row, pl.BlockSpec((tm, ne), lambda i: (i, 0))],
        out_shape=[jax.ShapeDtypeStruct((rows, d), F32),
                   jax.ShapeDtypeStruct((rows, ne), F32)],
        compiler_params=_cparams("parallel"),
        name="resid_ln_router",
    )(x, t, mods, ln_g.reshape(1, d), ln_b.reshape(1, d), w_router)


def _experts_kernel(idx_ref, x_hbm, mod_ref, wg_ref, wu_ref, wd_ref, g_ref, o_ref,
                    rows_ref, xe_ref, sem, *, rc, cap_lat):
    e = pl.program_id(0)
    f = pl.program_id(1)
    cap = xe_ref.shape[0]

    def start_gather(ee):
        def issue(r, carry):
            t = idx_ref[ee, r]
            pltpu.make_async_copy(x_hbm.at[pl.ds(t, 1), :], rows_ref.at[pl.ds(r, 1), :], sem).start()
            return carry
        lax.fori_loop(0, cap, issue, 0, unroll=8)

    @pl.when(jnp.logical_and(e == 0, f == 0))
    def _():
        start_gather(0)

    @pl.when(f == 0)
    def _():
        pltpu.make_async_copy(rows_ref, rows_ref, sem).wait()
        mod = mod_ref[...]
        for lo, hi, m in ((0, cap_lat, 0), (cap_lat, cap, 1)):
            if hi > lo:
                sh, sc = mod[m, 3:4, :], mod[m, 4:5, :]
                xe_ref[lo:hi, :] = (rows_ref[lo:hi, :] * (1.0 + sc) + sh).astype(BF16)
        o_ref[...] = jnp.zeros_like(o_ref)

        @pl.when(e + 1 < pl.num_programs(0))
        def _():
            start_gather(e + 1)

    wg = wg_ref[...].astype(BF16)
    wu = wu_ref[...].astype(BF16)
    wd = wd_ref[...].astype(BF16)
    for r0 in range(0, cap, rc):
        xe = xe_ref[r0:r0 + rc, :]
        hg = jnp.dot(xe, wg, preferred_element_type=F32)
        hu = jnp.dot(xe, wu, preferred_element_type=F32)
        h = (_silu(hg) * hu).astype(BF16)
        o_ref[r0:r0 + rc, :] += jnp.dot(h, wd, preferred_element_type=F32)

    @pl.when(f == pl.num_programs(1) - 1)
    def _():
        o_ref[...] = o_ref[...] * g_ref[...]


def _row_chunk(cap, limit=384, align=16):
    best = None
    for rc in range(align, min(cap, limit) + 1, align):
        if cap % rc == 0:
            best = rc
    assert best is not None, cap
    return best


def _experts_call(idx, xs, mods, w_gate, w_up, w_down, gsel, layer, cap_lat):
    ne, cap = idx.shape
    d = xs.shape[1]
    dexp = w_gate.shape[3]
    tf = 512 if dexp % 512 == 0 else dexp
    grid_spec = pltpu.PrefetchScalarGridSpec(
        num_scalar_prefetch=1,
        grid=(ne, dexp // tf),
        in_specs=[pl.BlockSpec(memory_space=pl.ANY),
                  pl.BlockSpec((2, N_MOD, d), lambda e, f, idx: (0, 0, 0)),
                  pl.BlockSpec((None, None, d, tf), lambda e, f, idx: (layer, e, 0, f)),
                  pl.BlockSpec((None, None, d, tf), lambda e, f, idx: (layer, e, 0, f)),
                  pl.BlockSpec((None, None, tf, d), lambda e, f, idx: (layer, e, f, 0)),
                  pl.BlockSpec((None, cap, 1), lambda e, f, idx: (e, 0, 0))],
        out_specs=pl.BlockSpec((None, cap, d), lambda e, f, idx: (e, 0, 0),
                               pipeline_mode=pl.Buffered(1)),
        scratch_shapes=[pltpu.VMEM((cap, d), F32), pltpu.VMEM((cap, d), BF16),
                        pltpu.SemaphoreType.DMA(())],
    )
    return pl.pallas_call(
        functools.partial(_experts_kernel, rc=_row_chunk(cap), cap_lat=cap_lat),
        grid_spec=grid_spec,
        out_shape=jax.ShapeDtypeStruct((ne, cap, d), F32),
        compiler_params=_cparams("arbitrary", "arbitrary"),
        name="experts",
    )(idx, xs, mods, w_gate, w_up, w_down, gsel)


def _route(aff, n_tok):
    cap = max(1, EC_CAPACITY_FACTOR * n_tok // N_EXPERTS)
    g, idx = lax.top_k(aff.T, cap)
    return g, idx


COMBINE_CHUNK = 256


def _combine_kernel(src_ref, bnd_ref, ye_hbm, tok_ref, x_ref, mod_ref, g_ref, b_ref, xo_ref,
                    buf_ref, acc_ref, cnt_ref, sem, *, alpha, n_chunks):
    b = pl.program_id(0)
    tb = x_ref.shape[0]
    cs = COMBINE_CHUNK

    def issue(kk):
        slot = kk % 2
        base = kk * cs

        def one(i, carry):
            pltpu.make_async_copy(ye_hbm.at[pl.ds(src_ref[base + i], 1), :],
                                  buf_ref.at[slot, pl.ds(i, 1), :], sem.at[slot]).start()
            return carry
        lax.fori_loop(0, cs, one, 0, unroll=8)
        cnt_ref[0] = kk + 1

    @pl.when(b == 0)
    def _():
        cnt_ref[0] = 0
        cnt_ref[1] = 0
        issue(0)

    lo, hi = bnd_ref[b], bnd_ref[b + 1]
    k0 = lo // cs
    k1 = jnp.where(hi > lo, (hi + cs - 1) // cs, k0)
    acc_ref[...] = jnp.zeros_like(acc_ref)
    row_tok = b * tb + lax.broadcasted_iota(jnp.int32, (tb, cs), 0)

    def chunk(kk, carry):
        slot = kk % 2

        @pl.when(kk >= cnt_ref[1])
        def _():
            pltpu.make_async_copy(buf_ref.at[slot], buf_ref.at[slot], sem.at[slot]).wait()
            cnt_ref[1] = kk + 1

        @pl.when(jnp.logical_and(kk + 1 < n_chunks, kk + 1 >= cnt_ref[0]))
        def _():
            issue(kk + 1)

        toks = tok_ref[:, pl.ds(pl.multiple_of(kk * cs, cs), cs)]
        onehot = jnp.where(row_tok == toks, 1.0, 0.0).astype(BF16)
        acc_ref[...] += jnp.dot(onehot, buf_ref[slot].astype(BF16), preferred_element_type=F32)
        return carry

    lax.fori_loop(k0, k1, chunk, 0)

    mod = mod_ref[...]
    xo_ref[...] = _resid_ln(x_ref[...], acc_ref[...], mod[5:6, :], g_ref[...], b_ref[...], alpha)


def _combine_call(ye, idx, xs1, mods, ln_g, ln_b, rows, n_lat_tiles, alpha):
    ne, cap, d = ye.shape
    tb = ROW_TILE
    cs = COMBINE_CHUNK
    nb = rows // tb
    n_list = ne * cap
    n_pad = -(-n_list // cs) * cs
    tok_flat = idx.reshape(-1)
    order = jnp.argsort(tok_flat).astype(jnp.int32)
    tok_sorted = jnp.take(tok_flat, order)
    bounds = jnp.searchsorted(tok_sorted, jnp.arange(nb + 1, dtype=jnp.int32) * tb).astype(jnp.int32)
    tok_pad = jnp.full((1, n_pad), -1, jnp.int32).at[0, :n_list].set(tok_sorted)
    src_pad = jnp.zeros((n_pad,), jnp.int32).at[:n_list].set(order)
    row = pl.BlockSpec((tb, d), lambda i, *_: (i, 0))
    vec = pl.BlockSpec((1, d), lambda i, *_: (0, 0))
    grid_spec = pltpu.PrefetchScalarGridSpec(
        num_scalar_prefetch=2,
        grid=(nb,),
        in_specs=[pl.BlockSpec(memory_space=pl.ANY),
                  pl.BlockSpec((1, n_pad), lambda i, *_: (0, 0)),
                  row,
                  pl.BlockSpec((None, N_MOD, d),
                               lambda i, *_: (jnp.where(i >= n_lat_tiles, 1, 0), 0, 0)),
                  vec, vec],
        out_specs=row,
        scratch_shapes=[pltpu.VMEM((2, cs, d), F32), pltpu.VMEM((tb, d), F32),
                        pltpu.SMEM((2,), jnp.int32), pltpu.SemaphoreType.DMA((2,))],
    )
    return pl.pallas_call(
        functools.partial(_combine_kernel, alpha=alpha, n_chunks=n_pad // cs),
        grid_spec=grid_spec,
        out_shape=jax.ShapeDtypeStruct((rows, d), F32),
        compiler_params=_cparams("arbitrary"),
        name="moe_combine_ln",
    )(src_pad, bounds, ye.reshape(n_list, d), tok_pad, xs1, mods,
      ln_g.reshape(1, d), ln_b.reshape(1, d))


def _spread_matrix(n_heads, direction):
    r = jnp.arange(4 * n_heads)[:, None] % (2 * n_heads)
    ch = jnp.arange(n_heads * HEAD_DIM)[None, :] // HEAD_DIM
    return (r == ch + direction * n_heads).astype(BF16)


def kernel(x, c, ctx, c_ctx, w_ada, b_ada, w_in, ssm_conv_w, ssm_conv_b, ssm_dt_bias, ssm_a_log,
           ssm_d, ssm_norm_w, w_ssm_out, conv_dw_w, conv_dw_b, conv_ln_g, conv_ln_b, w_conv_out,
           w_o, ln1_g, ln1_b, w_router, w_exp_gate, w_exp_up, w_exp_down, ln2_g, ln2_b):
    batch, n, d = x.shape
    nc = ctx.shape[1]
    depth = w_ada.shape[0]
    assert batch == 1 and n % ROW_TILE == 0 and nc % ROW_TILE == 0 and n % nc == 0
    d_inner = 2 * d
    n_heads = d_inner // HEAD_DIM
    d_bc = GROUPS * STATE
    d_xbc = d_inner + 2 * d_bc
    d_conv = d
    o_xbc = d_inner
    o_dt = o_xbc + d_xbc
    o_glu = o_dt + 2 * n_heads
    alpha = (2 * depth) ** 0.25
    nl, ncc = n // CHUNK, nc // CHUNK
    n_lat_tiles = n // ROW_TILE

    xs = jnp.concatenate([x[0], ctx[0]], axis=0)
    cvec = jnp.zeros((SUBLANES, d), F32).at[0].set(c[0]).at[1].set(c_ctx)
    mods_all = _ada_call(cvec, w_ada, b_ada)
    e_mats = [_spread_matrix(n_heads, dr) for dr in range(2)]

    for i in range(depth):
        last = i == depth - 1
        mods = mods_all[i, :2].reshape(2, N_MOD, d)
        rows_all = n + nc
        rows = n if last else rows_all

        u_in = _modulate_call(xs, mods, rows_all, n_lat_tiles)
        z = _mm_call(u_in, w_in, i, 0, d_inner, rows=rows, name="proj_z")
        xbc = _mm_call(u_in, w_in, i, o_xbc, d_xbc, rows=rows_all, name="proj_xbc",
                       slab_major=True)
        dtraw = _mm_call(u_in, w_in, i, o_dt, 2 * n_heads, rows=rows_all, name="proj_dt")
        glu = _mm_call(u_in, w_in, i, o_glu, 2 * d_conv + 2 * d, rows=rows, name="proj_glu_gate")

        conv_w = jnp.zeros((SUBLANES, d_xbc), F32).at[:SSM_CONV].set(ssm_conv_w[i])
        conv_b = ssm_conv_b[i].reshape(1, d_xbc)
        dt_bias = ssm_dt_bias[i].reshape(1, 2 * n_heads)
        a_row = -jnp.exp(ssm_a_log[i].astype(F32)).reshape(1, 2 * n_heads)
        d_row = jnp.repeat(ssm_d[i], HEAD_DIM).reshape(1, d_inner)
        y = _ssd_call(xbc, dtraw, conv_w, conv_b, dt_bias, a_row, e_mats, d_row,
                      nl=nl, ncc=ncc, d_inner=d_inner)
        a_act = _gnorm_call(y, z, ssm_norm_w[i].reshape(1, d_inner), rows, d_inner)

        hcv = _glu_conv_call(glu, conv_dw_w[i], conv_dw_b[i], n=n, nc=rows - n, d_conv=d_conv)
        b_act = _lnswish_call([hcv], conv_ln_g[i], conv_ln_b[i])

        t1 = _mm_call(a_act, w_ssm_out, i, 0, d, rows=rows, name="ssm_out",
                      epi_fn=lambda acc, gt: _sigmoid(gt) * acc,
                      epi_cols=[2 * d_conv], epi_args=[glu])
        t2 = _mm_call(b_act, w_conv_out, i, 0, d, rows=rows, name="conv_out", out_dtype=BF16,
                      epi_fn=lambda acc, gt, prev: prev + _sigmoid(gt) * acc,
                      epi_cols=[2 * d_conv + d, 0], epi_args=[glu, t1])
        mix = _mm_call(t2, w_o, i, 0, d, rows=rows, name="w_o")

        xs1, aff = _resid_call(xs, mix, mods, ln1_g[i], ln1_b[i], rows, n_lat_tiles, alpha,
                               w_router=w_router[i])

        g_l, idx_l = _route(aff[:n], n)
        gsel, idx = g_l, idx_l
        if not last:
            g_c, idx_c = _route(aff[n:], nc)
            gsel = jnp.concatenate([g_l, g_c], axis=1)
            idx = jnp.concatenate([idx_l, idx_c + n], axis=1)
        ye = _experts_call(idx, xs1, mods, w_exp_gate, w_exp_up, w_exp_down, gsel[..., None], i,
                           idx_l.shape[1])
        xs = _combine_call(ye, idx, xs1, mods, ln2_g[i], ln2_b[i], rows, n_lat_tiles, alpha)

    return xs[:n].reshape(1, n, d)
```

```python
import functools

import jax
import jax.numpy as jnp
from jax import lax
from jax.experimental import pallas as pl
from jax.experimental.pallas import tpu as pltpu

F32 = jnp.float32
BF16 = jnp.bfloat16

HEAD_DIM = 64
GROUPS = 8
STATE = 128
SSM_CONV = 5
CHUNK = 128
CONV_K = 31
GRID_W = 64
N_EXPERTS = 16
EC_CAPACITY_FACTOR = 2
N_MOD = 6
LN_EPS = 1e-5

LANES = 128
SUBLANES = 8
VMEM_LIMIT = 60 * 1024 * 1024
MM_VMEM_BUDGET = 47 * 1024 * 1024
MXU_FLOPS = 1.1e15
MXU_WEIGHT_ROWS = 256
HBM_BYTES_PER_S = 3.3e12
GRID_STEP_S = 0.35e-6

ROW_TILE = 256
HALO = SUBLANES


def _cparams(*sem):
    return pltpu.CompilerParams(dimension_semantics=sem, vmem_limit_bytes=VMEM_LIMIT)


def _sigmoid(v):
    return jax.nn.sigmoid(v)


def _silu(v):
    return v * jax.nn.sigmoid(v)


def _softplus(v):
    return jnp.maximum(v, 0.0) + jnp.log1p(jnp.exp(-jnp.abs(v)))


def _ada_kernel(c_ref, w_ref, b_ref, o_ref):
    a = _silu(c_ref[...])
    o_ref[...] = jnp.dot(a, w_ref[...], preferred_element_type=F32,
                         precision=lax.Precision.HIGHEST) + b_ref[...]


def _ada_call(cvec, w_ada, b_ada):
    depth, d, nm = w_ada.shape
    tn = 1024 if nm % 1024 == 0 else nm
    return pl.pallas_call(
        _ada_kernel,
        grid=(depth, nm // tn),
        in_specs=[
            pl.BlockSpec((SUBLANES, d), lambda l, j: (0, 0)),
            pl.BlockSpec((None, d, tn), lambda l, j: (l, 0, j)),
            pl.BlockSpec((None, 1, tn), lambda l, j: (l, 0, j)),
        ],
        out_specs=pl.BlockSpec((None, SUBLANES, tn), lambda l, j: (l, 0, j)),
        out_shape=jax.ShapeDtypeStruct((depth, SUBLANES, nm), F32),
        compiler_params=_cparams("arbitrary", "arbitrary"),
        name="adaln",
    )(cvec, w_ada, b_ada.reshape(depth, 1, nm))


def _mm_kernel(*refs, n_epi, epi_fn, tm):
    x_ref, w_ref = refs[0], refs[1]
    epi = refs[2:2 + n_epi]
    o_ref = refs[2 + n_epi]
    wb_ref = refs[3 + n_epi]
    i = pl.program_id(1)

    @pl.when(i == 0)
    def _():
        wb_ref[...] = w_ref[...].astype(BF16)

    if x_ref.shape[0] == tm:
        x = x_ref[...]
    else:
        x = x_ref[pl.ds(pl.multiple_of(i * tm, tm), tm), :]
    acc = jnp.dot(x, wb_ref[...], preferred_element_type=F32)
    if epi_fn is not None:
        acc = epi_fn(acc, *[r[...] for r in epi])
    if len(o_ref.shape) == 3:
        for jj in range(o_ref.shape[0]):
            o_ref[jj] = acc[:, jj * LANES:(jj + 1) * LANES].astype(o_ref.dtype)
    else:
        o_ref[...] = acc.astype(o_ref.dtype)


def _mm_tiles(rows, k, n_out, n_epi, out_bytes):
    best = None
    for tm in range(LANES, rows + 1, LANES):
        if rows % tm:
            continue
        for tn in (1024, 512, 256, 128):
            if n_out % tn:
                continue
            for resident in (False, True):
                x_vmem = rows * k * 2 if resident else 2 * tm * k * 2
                vmem = (x_vmem + 2 * k * tn * 4 + k * tn * 2 + 2 * tm * tn * out_bytes
                        + n_epi * 3 * tm * tn * 4 + tm * tn * 4)
                if vmem > MM_VMEM_BUDGET:
                    continue
                x_reads = 1 if resident else n_out // tn
                t_mxu = 2.0 * rows * k * n_out / MXU_FLOPS * (1.0 + MXU_WEIGHT_ROWS / tm)
                if tn < MXU_WEIGHT_ROWS:
                    t_mxu *= 2.0
                t_hbm = (rows * k * 2.0 * x_reads + k * n_out * 4.0
                         + rows * n_out * (out_bytes + 4.0 * n_epi)) / HBM_BYTES_PER_S
                t = t_mxu + 0.5 * t_hbm + (rows // tm) * (n_out // tn) * GRID_STEP_S
                if best is None or t < best[0]:
                    best = (t, tm, tn, resident)
    assert best is not None
    return best[1:]


def _mm_call(x, w, layer, col_off, n_out, *, rows, name, out_dtype=F32,
             epi_fn=None, epi_cols=(), epi_args=(), slab_major=False):
    k = x.shape[1]
    tm, tn, resident = _mm_tiles(rows, k, n_out, len(epi_args), jnp.dtype(out_dtype).itemsize)
    assert col_off % LANES == 0
    kern = functools.partial(_mm_kernel, n_epi=len(epi_args), epi_fn=epi_fn, tm=tm)
    if resident and rows > tm:
        x_spec = pl.BlockSpec((rows, k), lambda j, i: (0, 0), pipeline_mode=pl.Buffered(1))
    else:
        x_spec = pl.BlockSpec((tm, k), lambda j, i: (i, 0))
    epi_specs = [pl.BlockSpec((tm, tn), functools.partial(
        lambda j, i, off: (i, j + off), off=c // tn)) for c in epi_cols]
    assert all(c % tn == 0 for c in epi_cols)
    if slab_major:
        out_spec = pl.BlockSpec((tn // LANES, tm, LANES), lambda j, i: (j, i, 0))
        out_shape = jax.ShapeDtypeStruct((n_out // LANES, rows, LANES), out_dtype)
    else:
        out_spec = pl.BlockSpec((tm, tn), lambda j, i: (i, j))
        out_shape = jax.ShapeDtypeStruct((rows, n_out), out_dtype)
    return pl.pallas_call(
        kern,
        grid=(n_out // tn, rows // tm),
        in_specs=[x_spec,
                  pl.BlockSpec((pl.Element(k), pl.Element(tn)),
                               lambda j, i: (layer * k, pl.multiple_of(col_off + j * tn, LANES))),
                  *epi_specs],
        out_specs=out_spec,
        out_shape=out_shape,
        scratch_shapes=[pltpu.VMEM((k, tn), BF16)],
        compiler_params=_cparams("arbitrary", "arbitrary"),
        name=name,
    )(x, w.reshape(-1, w.shape[-1]), *epi_args)


def _modulate_kernel(x_ref, mod_ref, o_ref):
    mod = mod_ref[...]
    o_ref[...] = (x_ref[...] * (1.0 + mod[1:2, :]) + mod[0:1, :]).astype(o_ref.dtype)


def _modulate_call(xs, mods, rows, n_lat_tiles):
    d = xs.shape[1]
    tm = ROW_TILE
    return pl.pallas_call(
        _modulate_kernel,
        grid=(rows // tm,),
        in_specs=[pl.BlockSpec((tm, d), lambda i: (i, 0)),
                  pl.BlockSpec((None, N_MOD, d),
                               lambda i: (jnp.where(i >= n_lat_tiles, 1, 0), 0, 0))],
        out_specs=pl.BlockSpec((tm, d), lambda i: (i, 0)),
        out_shape=jax.ShapeDtypeStruct((rows, d), BF16),
        compiler_params=_cparams("parallel"),
        name="modulate",
    )(xs, mods)


def _ssd_chunk_of_step(s, direction, nl, ncc):
    if direction == 0:
        return jnp.where(s < ncc, nl + s, s - ncc)
    return jnp.where(s < ncc, nl + ncc - 1 - s, nl - 1 - (s - ncc))


def _ssd_kernel(*refs, direction, nl, ncc, d_inner):
    if direction == 0:
        (xm_ref, xp_ref, xn_ref, dt_ref, cw_ref, cb_ref, dtb_ref, arow_ref, e_ref, dsk_ref,
         y_ref, xc_ref, state_ref, ext_ref) = refs
        yprev_ref = None
    else:
        (xc_ref, dt_ref, dtb_ref, arow_ref, e_ref, yprev_ref, y_ref, state_ref, ext_ref,
         ys_ref) = refs
    q = CHUNK
    d_bc = GROUPS * STATE
    d_xbc = d_inner + 2 * d_bc
    gw = d_inner // GROUPS
    hpg = gw // HEAD_DIM
    n_heads = d_inner // HEAD_DIM
    n_grp = q // SUBLANES

    s = pl.program_id(0)
    c = _ssd_chunk_of_step(s, direction, nl, ncc)

    @pl.when(s == 0)
    def _():
        state_ref[...] = jnp.zeros_like(state_ref)

    def slabs(ref3, rows):
        return jnp.concatenate([ref3[j, rows, :] for j in range(ref3.shape[0])], axis=1)

    if direction == 0:
        pad = SSM_CONV // 2
        seg_first = jnp.logical_or(c == 0, c == nl)
        seg_last = jnp.logical_or(c == nl - 1, c == nl + ncc - 1)
        every = slice(None)
        prev = jnp.where(seg_first, 0.0, slabs(xp_ref, every))
        nxt = jnp.where(seg_last, 0.0, slabs(xn_ref, every))
        sub = lax.broadcasted_iota(jnp.int32, (SUBLANES, d_xbc), 0)
        for m in range(n_grp):
            grp = slabs(xm_ref, pl.ds(m, SUBLANES, stride=n_grp))
            ext_ref[(pad + m) * SUBLANES:(pad + m + 1) * SUBLANES, :] = grp
            if m >= n_grp - pad:
                j = m - (n_grp - pad)
                halo = prev[HALO - pad + j:HALO - pad + j + 1, :]
                ext_ref[j * SUBLANES:(j + 1) * SUBLANES, :] = jnp.where(
                    sub == 0, halo, pltpu.roll(grp, 1, axis=0))
            if m < pad:
                halo = nxt[m:m + 1, :]
                ext_ref[(pad + n_grp + m) * SUBLANES:(pad + n_grp + m + 1) * SUBLANES, :] = (
                    jnp.where(sub == SUBLANES - 1, halo, pltpu.roll(grp, SUBLANES - 1, axis=0)))
        ct = 512
        for j in range(d_xbc // ct):
            cs = slice(j * ct, (j + 1) * ct)
            acc = jnp.broadcast_to(cb_ref[:, cs], (q, ct))
            for k in range(SSM_CONV):
                acc = acc + cw_ref[k:k + 1, cs] * ext_ref[k * SUBLANES:k * SUBLANES + q, cs]
            xc_ref[:, cs] = _silu(acc)

    dt_raw = jnp.concatenate([dt_ref[pl.ds(m, SUBLANES, stride=n_grp), :] for m in range(n_grp)],
                             axis=0)
    dt = _softplus(dt_raw + dtb_ref[...])
    a = dt * arow_ref[...]
    ri = lax.broadcasted_iota(jnp.int32, (q, q), 0)
    ci = lax.broadcasted_iota(jnp.int32, (q, q), 1)
    tok_i = (ri % SUBLANES) * n_grp + ri // SUBLANES
    tok_j = (ci % SUBLANES) * n_grp + ci // SUBLANES
    causal = (tok_i >= tok_j) if direction == 0 else (tok_i <= tok_j)
    cum = jnp.dot(causal.astype(F32), a, preferred_element_type=F32,
                  precision=lax.Precision.HIGHEST)
    tot = cum[q - 1:q, :] if direction == 0 else cum[0:1, :]
    e_in = jnp.exp(cum)
    w_st = dt * jnp.exp(tot - cum)
    cum_t = cum.T
    dt_t = dt.T

    stack = jnp.concatenate([w_st, e_in], axis=0)
    hi = stack.astype(BF16)
    lo = (stack - hi.astype(F32)).astype(BF16)
    spread = jnp.dot(jnp.concatenate([hi, lo], axis=1), e_ref[...],
                     preferred_element_type=F32)
    ext_ref[0:2 * q, 0:d_inner] = spread
    dec_row = q - 1 if direction == 0 else 0

    lane = lax.broadcasted_iota(jnp.int32, (q, LANES), 1)
    first_head = lane < HEAD_DIM
    hoff = direction * n_heads

    for g in range(GROUPS):
        bg = xc_ref[:, d_inner + g * STATE:d_inner + (g + 1) * STATE]
        cg = xc_ref[:, d_inner + d_bc + g * STATE:d_inner + d_bc + (g + 1) * STATE]
        bgb = bg.astype(BF16)
        cgb = cg.astype(BF16)
        scores = lax.dot_general(cgb, bgb, (((1,), (1,)), ((), ())),
                                 preferred_element_type=F32)
        gs = slice(g * gw, (g + 1) * gw)
        st = state_ref[:, gs]
        y_g = jnp.dot(cgb, st.astype(BF16), preferred_element_type=F32) * ext_ref[q:2 * q, gs]
        xg = xc_ref[:, gs]
        xw = (xg * ext_ref[0:q, gs]).astype(BF16)
        s_new = jnp.dot(bg.T.astype(BF16), xw, preferred_element_type=F32)
        state_ref[:, gs] = st * ext_ref[q + dec_row:q + dec_row + 1, gs] + s_new
        parts = []
        for p in range(hpg // 2):
            slab = xg[:, p * LANES:(p + 1) * LANES]
            y_p = None
            for half in range(2):
                h = hoff + g * hpg + 2 * p + half
                seg = cum[:, h:h + 1] - cum_t[h:h + 1, :]
                m = scores * jnp.exp(jnp.where(causal, seg, -jnp.inf)) * dt_t[h:h + 1, :]
                keep = first_head if half == 0 else jnp.logical_not(first_head)
                xh = jnp.where(keep, slab, 0.0).astype(BF16)
                t = jnp.dot(m.astype(BF16), xh, preferred_element_type=F32)
                y_p = t if y_p is None else y_p + t
            parts.append(y_p)
        y_g = y_g + jnp.concatenate(parts, axis=1)
        if direction == 0:
            y_ref[:, gs] = y_g + dsk_ref[:, gs] * xg
        else:
            y_g = y_g + yprev_ref[:, gs]
            for jl in range(gw // LANES):
                ys_ref[...] = y_g[:, jl * LANES:(jl + 1) * LANES]
                jg = g * (gw // LANES) + jl
                for a in range(n_grp):
                    start = (a % 2) * (q // 2) + a // 2
                    y_ref[jg, a * SUBLANES:(a + 1) * SUBLANES, :] = (
                        ys_ref[pl.ds(start, SUBLANES, stride=SUBLANES), :])


def _ssd_call(xbc, dtraw, conv_w, conv_b, dt_bias, a_row, e_mats, d_row, *, nl, ncc, d_inner):
    rows = (nl + ncc) * CHUNK
    d_xbc = d_inner + 2 * GROUPS * STATE
    nh2 = dtraw.shape[1]
    q = CHUNK
    qh = q // HALO
    nchunks = nl + ncc
    full = lambda shape: pl.BlockSpec(shape, lambda s: (0,) * len(shape))
    scratch = [pltpu.VMEM((STATE, d_inner), F32), pltpu.VMEM((2 * q, d_xbc), F32)]

    def chunk_spec(width, direction):
        cmap = functools.partial(_ssd_chunk_of_step, direction=direction, nl=nl, ncc=ncc)
        return pl.BlockSpec((q, width), lambda s: (cmap(s), 0))

    cmap = functools.partial(_ssd_chunk_of_step, direction=0, nl=nl, ncc=ncc)
    cmap_b = functools.partial(_ssd_chunk_of_step, direction=1, nl=nl, ncc=ncc)
    ns_x, ns_y = d_xbc // LANES, d_inner // LANES
    main_spec = pl.BlockSpec((ns_x, q, LANES), lambda s: (0, cmap(s), 0))
    prev_spec = pl.BlockSpec((ns_x, HALO, LANES),
                             lambda s: (0, jnp.maximum(cmap(s) * qh - 1, 0), 0))
    next_spec = pl.BlockSpec((ns_x, HALO, LANES),
                             lambda s: (0, jnp.minimum((cmap(s) + 1) * qh, nchunks * qh - 1), 0))
    y_fwd, xc = pl.pallas_call(
        functools.partial(_ssd_kernel, direction=0, nl=nl, ncc=ncc, d_inner=d_inner),
        grid=(nchunks,),
        in_specs=[main_spec, prev_spec, next_spec, chunk_spec(nh2, 0),
                  full((SUBLANES, d_xbc)), full((1, d_xbc)), full((1, nh2)), full((1, nh2)),
                  full(e_mats[0].shape), full((1, d_inner))],
        out_specs=[chunk_spec(d_inner, 0), chunk_spec(d_xbc, 0)],
        out_shape=[jax.ShapeDtypeStruct((rows, d_inner), F32),
                   jax.ShapeDtypeStruct((rows, d_xbc), F32)],
        scratch_shapes=scratch,
        compiler_params=_cparams("arbitrary"),
        name="ssd_fwd",
    )(xbc, xbc, xbc, dtraw, conv_w, conv_b, dt_bias, a_row, e_mats[0], d_row)
    return pl.pallas_call(
        functools.partial(_ssd_kernel, direction=1, nl=nl, ncc=ncc, d_inner=d_inner),
        grid=(nchunks,),
        in_specs=[chunk_spec(d_xbc, 1), chunk_spec(nh2, 1), full((1, nh2)), full((1, nh2)),
                  full(e_mats[1].shape), chunk_spec(d_inner, 1)],
        out_specs=pl.BlockSpec((ns_y, q, LANES), lambda s: (0, cmap_b(s), 0)),
        out_shape=jax.ShapeDtypeStruct((ns_y, rows, LANES), F32),
        scratch_shapes=scratch + [pltpu.VMEM((q, LANES), F32)],
        compiler_params=_cparams("arbitrary"),
        name="ssd_bwd",
    )(xc, dtraw, dt_bias, a_row, e_mats[1], y_fwd)


def _gnorm_kernel(y_ref, z_ref, w_ref, o_ref, *, d_inner):
    gw = d_inner // GROUPS
    spg = gw // LANES
    for g in range(GROUPS):
        gs = slice(g * gw, (g + 1) * gw)
        y = jnp.concatenate([y_ref[g * spg + j] for j in range(spg)], axis=1)
        h = y * _silu(z_ref[:, gs])
        ms = jnp.mean(h * h, axis=-1, keepdims=True)
        o_ref[:, gs] = (h * lax.rsqrt(ms + LN_EPS) * w_ref[:, gs]).astype(o_ref.dtype)


def _gnorm_call(y, zx, norm_w, rows, d_inner):
    tm = ROW_TILE
    return pl.pallas_call(
        functools.partial(_gnorm_kernel, d_inner=d_inner),
        grid=(rows // tm,),
        in_specs=[pl.BlockSpec((d_inner // LANES, tm, LANES), lambda i: (0, i, 0)),
                  pl.BlockSpec((tm, d_inner), lambda i: (i, 0)),
                  pl.BlockSpec((1, d_inner), lambda i: (0, 0))],
        out_specs=pl.BlockSpec((tm, d_inner), lambda i: (i, 0)),
        out_shape=jax.ShapeDtypeStruct((rows, d_inner), BF16),
        compiler_params=_cparams("parallel"),
        name="gated_rmsnorm",
    )(y, zx, norm_w)


CONV_GAP = 2 * SUBLANES


def _conv_runs(ga_ref, gb_ref, w_ref, b_ref, o_ref, pad_ref, *, row0, seq, n_seq):
    half = CONV_K // 2
    gap = CONV_GAP
    rb = pad_ref.shape[0]
    pad_ref[:, 0:gap, :] = jnp.zeros((rb, gap, LANES), F32)
    pad_ref[:, gap + seq:gap + seq + gap, :] = jnp.zeros((rb, gap, LANES), F32)
    bias = b_ref[...]

    def body(it, carry):
        r0 = pl.multiple_of(row0 + it * (rb * seq), SUBLANES)
        ga = ga_ref[pl.ds(r0, rb * seq), :]
        gb = gb_ref[pl.ds(r0, rb * seq), :]
        pad_ref[:, gap:gap + seq, :] = (ga * _sigmoid(gb)).reshape(rb, seq, LANES)
        acc = jnp.broadcast_to(bias.reshape(1, 1, LANES), (rb, seq, LANES))
        for k in range(CONV_K):
            o = gap - half + k
            acc = acc + w_ref[k:k + 1, :].reshape(1, 1, LANES) * pad_ref[:, o:o + seq, :]
        o_ref[pl.ds(r0, rb * seq), :] = acc.reshape(rb * seq, LANES)
        return carry

    lax.fori_loop(0, n_seq // rb, body, 0)


def _conv_stride(ga_ref, gb_ref, w_ref, b_ref, o_ref, pad_ref, *, n, stride):
    half = CONV_K // 2
    halo = half * stride
    pad_ref[0:halo, :] = jnp.zeros((halo, LANES), F32)
    pad_ref[halo + n:halo + n + halo, :] = jnp.zeros((halo, LANES), F32)
    tb = 256
    bias = b_ref[...]

    def fill(it, carry):
        r0 = pl.multiple_of(it * tb, tb)
        pad_ref[pl.ds(halo + r0, tb), :] = ga_ref[pl.ds(r0, tb), :] * _sigmoid(gb_ref[pl.ds(r0, tb), :])
        return carry

    lax.fori_loop(0, n // tb, fill, 0)

    def body(it, carry):
        r0 = pl.multiple_of(it * tb, tb)
        acc = jnp.broadcast_to(bias, (tb, LANES))
        for k in range(CONV_K):
            acc = acc + w_ref[k:k + 1, :] * pad_ref[pl.ds(r0 + k * stride, tb), :]
        o_ref[pl.ds(r0, tb), :] = acc
        return carry

    lax.fori_loop(0, n // tb, body, 0)


def _glu_conv_kernel(ga_ref, gb_ref, w_ref, b_ref, o_ref, pad_rows, pad_cols, pad_ctx,
                     *, n, nc, n_row_tiles):
    c = pl.program_id(0)

    @pl.when(c < n_row_tiles)
    def _():
        _conv_runs(ga_ref, gb_ref, w_ref, b_ref, o_ref, pad_rows, row0=0, seq=GRID_W,
                   n_seq=n // GRID_W)

    @pl.when(c >= n_row_tiles)
    def _():
        _conv_stride(ga_ref, gb_ref, w_ref, b_ref, o_ref, pad_cols, n=n, stride=GRID_W)

    if nc:
        _conv_runs(ga_ref, gb_ref, w_ref, b_ref, o_ref, pad_ctx, row0=n, seq=nc, n_seq=1)


def _glu_conv_call(glu, conv_w, conv_b, *, n, nc, d_conv):
    rows = n + nc
    half = d_conv // 2
    assert half % LANES == 0 and n % (4 * GRID_W) == 0
    gblk = d_conv // LANES
    wpad = jnp.zeros((32, d_conv), F32).at[:CONV_K].set(conv_w)
    bias = conv_b.reshape(1, d_conv)
    kern = functools.partial(_glu_conv_kernel, n=n, nc=nc, n_row_tiles=half // LANES)
    return pl.pallas_call(
        kern,
        grid=(d_conv // LANES,),
        in_specs=[pl.BlockSpec((rows, LANES), lambda c: (0, c)),
                  pl.BlockSpec((rows, LANES), lambda c: (0, gblk + c)),
                  pl.BlockSpec((32, LANES), lambda c: (0, c)),
                  pl.BlockSpec((1, LANES), lambda c: (0, c))],
        out_specs=pl.BlockSpec((rows, LANES), lambda c: (0, c)),
        out_shape=jax.ShapeDtypeStruct((rows, d_conv), F32),
        scratch_shapes=[pltpu.VMEM((4, GRID_W + 2 * CONV_GAP, LANES), F32),
                        pltpu.VMEM((n + 2 * (CONV_K // 2) * GRID_W, LANES), F32),
                        pltpu.VMEM((1, max(nc, SUBLANES) + 2 * CONV_GAP, LANES), F32)],
        compiler_params=_cparams("parallel"),
        name="glu_conv",
    )(glu, glu, wpad, bias)


def _lnswish_kernel(*refs, n_in):
    ins = refs[:n_in]
    g_ref, b_ref, o_ref = refs[n_in:]
    x = jnp.concatenate([r[...] for r in ins], axis=1) if n_in > 1 else ins[0][...]
    mu = jnp.mean(x, axis=-1, keepdims=True)
    xc = x - mu
    var = jnp.mean(xc * xc, axis=-1, keepdims=True)
    y = xc * lax.rsqrt(var + LN_EPS) * g_ref[...] + b_ref[...]
    o_ref[...] = _silu(y).astype(o_ref.dtype)


def _lnswish_call(parts, ln_g, ln_b):
    rows = parts[0].shape[0]
    d = sum(p.shape[1] for p in parts)
    tm = ROW_TILE
    return pl.pallas_call(
        functools.partial(_lnswish_kernel, n_in=len(parts)),
        grid=(rows // tm,),
        in_specs=[*[pl.BlockSpec((tm, p.shape[1]), lambda i: (i, 0)) for p in parts],
                  pl.BlockSpec((1, d), lambda i: (0, 0)),
                  pl.BlockSpec((1, d), lambda i: (0, 0))],
        out_specs=pl.BlockSpec((tm, d), lambda i: (i, 0)),
        out_shape=jax.ShapeDtypeStruct((rows, d), BF16),
        compiler_params=_cparams("parallel"),
        name="conv_ln_swish",
    )(*parts, ln_g.reshape(1, d), ln_b.reshape(1, d))


def _resid_ln(x, t, gate, g, b, alpha):
    v = alpha * x + gate * t
    mu = jnp.mean(v, axis=-1, keepdims=True)
    vc = v - mu
    var = jnp.mean(vc * vc, axis=-1, keepdims=True)
    return vc * lax.rsqrt(var + LN_EPS) * g + b


def _resid_ln_router_kernel(x_ref, t_ref, mod_ref, g_ref, b_ref, wr_ref, xo_ref, aff_ref,
                            *, alpha):
    mod = mod_ref[...]
    xn = _resid_ln(x_ref[...], t_ref[...], mod[2:3, :], g_ref[...], b_ref[...], alpha)
    xo_ref[...] = xn
    u = xn * (1.0 + mod[4:5, :]) + mod[3:4, :]
    logits = jnp.dot(u, wr_ref[...], preferred_element_type=F32, precision=lax.Precision.HIGHEST)
    mx = jnp.max(logits, axis=-1, keepdims=True)
    ex = jnp.exp(logits - mx)
    aff_ref[...] = ex / jnp.sum(ex, axis=-1, keepdims=True)


def _resid_ln_kernel(x_ref, t_ref, mod_ref, g_ref, b_ref, xo_ref, *, alpha):
    mod = mod_ref[...]
    xo_ref[...] = _resid_ln(x_ref[...], t_ref[...], mod[5:6, :], g_ref[...], b_ref[...], alpha)


def _resid_call(x, t, mods, ln_g, ln_b, rows, n_lat_tiles, alpha, w_router=None):
    d = x.shape[1]
    tm = ROW_TILE
    row = pl.BlockSpec((tm, d), lambda i: (i, 0))
    vec = pl.BlockSpec((1, d), lambda i: (0, 0))
    mod_spec = pl.BlockSpec((None, N_MOD, d), lambda i: (jnp.where(i >= n_lat_tiles, 1, 0), 0, 0))
    if w_router is None:
        return pl.pallas_call(
            functools.partial(_resid_ln_kernel, alpha=alpha),
            grid=(rows // tm,),
            in_specs=[row, row, mod_spec, vec, vec],
            out_specs=row,
            out_shape=jax.ShapeDtypeStruct((rows, d), F32),
            compiler_params=_cparams("parallel"),
            name="resid_ln",
        )(x, t, mods, ln_g.reshape(1, d), ln_b.reshape(1, d))
    ne = w_router.shape[1]
    return pl.pallas_call(
        functools.partial(_resid_ln_router_kernel, alpha=alpha),
        grid=(rows // tm,),
        in_specs=[row, row, mod_spec, vec, vec, pl.BlockSpec((d, ne), lambda i: (0, 0))],
        out_specs=[row, pl.BlockSpec((tm, ne), lambda i: (i, 0))],
        out_shape=[jax.ShapeDtypeStruct((rows, d), F32),
                   jax.ShapeDtypeStruct((rows, ne), F32)],
        compiler_params=_cparams("parallel"),
        name="resid_ln_router",
    )(x, t, mods, ln_g.reshape(1, d), ln_b.reshape(1, d), w_router)


def _experts_kernel(idx_ref, x_hbm, mod_ref, wg_ref, wu_ref, wd_ref, g_ref, o_ref,
                    rows_ref, xe_ref, sem, *, rc, cap_lat, nf):
    e = pl.program_id(0)
    f = pl.program_id(1)
    cap = xe_ref.shape[0]
    ne = pl.num_programs(0)
    per = _gather_rows_per_step(cap, nf)
    assert nf * per - cap <= rows_ref.shape[0] - cap

    def issue_row(ee, k):
        real = k < cap
        t = idx_ref[ee, jnp.where(real, k, cap - 1)]
        pltpu.make_async_copy(x_hbm.at[pl.ds(t, 1), :], rows_ref.at[pl.ds(k, 1), :], sem).start()

    def wait_gather():
        full = rows_ref.at[pl.ds(0, nf * per), :]
        pltpu.make_async_copy(full, full, sem).wait()

    @pl.when(jnp.logical_and(e == 0, f == 0))
    def _():
        def issue(k, carry):
            issue_row(0, k)
            return carry
        lax.fori_loop(0, nf * per, issue, 0, unroll=8)

    @pl.when(f == 0)
    def _():
        wait_gather()
        mod = mod_ref[...]
        for lo, hi, m in ((0, cap_lat, 0), (cap_lat, cap, 1)):
            if hi > lo:
                sh, sc = mod[m, 3:4, :], mod[m, 4:5, :]
                xe_ref[lo:hi, :] = (rows_ref[lo:hi, :] * (1.0 + sc) + sh).astype(BF16)
        o_ref[...] = jnp.zeros_like(o_ref)

    e_next = jnp.where(e + 1 < ne, e + 1, 0)
    for r in range(per):
        issue_row(e_next, f * per + r)

    wg = wg_ref[...].astype(BF16)
    wu = wu_ref[...].astype(BF16)
    wd = wd_ref[...].astype(BF16)
    for r0 in range(0, cap, rc):
        xe = xe_ref[r0:r0 + rc, :]
        hg = jnp.dot(xe, wg, preferred_element_type=F32)
        hu = jnp.dot(xe, wu, preferred_element_type=F32)
        h = (_silu(hg) * hu).astype(BF16)
        o_ref[r0:r0 + rc, :] += jnp.dot(h, wd, preferred_element_type=F32)

    @pl.when(f == nf - 1)
    def _():
        o_ref[...] = o_ref[...] * g_ref[...]

        @pl.when(e == ne - 1)
        def _():
            wait_gather()


def _gather_rows_per_step(cap, nf):
    per = -(-cap // nf)
    while (nf * per) % SUBLANES:
        per += 1
    return per


def _row_chunk(cap, limit=384, align=16):
    best = None
    for rc in range(align, min(cap, limit) + 1, align):
        if cap % rc == 0:
            best = rc
    assert best is not None, cap
    return best


def _experts_call(idx, xs, mods, w_gate, w_up, w_down, gsel, layer, cap_lat):
    ne, cap = idx.shape
    d = xs.shape[1]
    dexp = w_gate.shape[3]
    tf = 512 if dexp % 512 == 0 else dexp
    nf = dexp // tf
    grid_spec = pltpu.PrefetchScalarGridSpec(
        num_scalar_prefetch=1,
        grid=(ne, dexp // tf),
        in_specs=[pl.BlockSpec(memory_space=pl.ANY),
                  pl.BlockSpec((2, N_MOD, d), lambda e, f, idx: (0, 0, 0)),
                  pl.BlockSpec((None, None, d, tf), lambda e, f, idx: (layer, e, 0, f)),
                  pl.BlockSpec((None, None, d, tf), lambda e, f, idx: (layer, e, 0, f)),
                  pl.BlockSpec((None, None, tf, d), lambda e, f, idx: (layer, e, f, 0)),
                  pl.BlockSpec((None, cap, 1), lambda e, f, idx: (e, 0, 0))],
        out_specs=pl.BlockSpec((None, cap, d), lambda e, f, idx: (e, 0, 0),
                               pipeline_mode=pl.Buffered(1)),
        scratch_shapes=[pltpu.VMEM((nf * _gather_rows_per_step(cap, nf), d), F32),
                        pltpu.VMEM((cap, d), BF16), pltpu.SemaphoreType.DMA(())],
    )
    return pl.pallas_call(
        functools.partial(_experts_kernel, rc=_row_chunk(cap), cap_lat=cap_lat, nf=nf),
        grid_spec=grid_spec,
        out_shape=jax.ShapeDtypeStruct((ne, cap, d), F32),
        compiler_params=_cparams("arbitrary", "arbitrary"),
        name="experts",
    )(idx, xs, mods, w_gate, w_up, w_down, gsel)


def _route(aff, n_tok):
    cap = max(1, EC_CAPACITY_FACTOR * n_tok // N_EXPERTS)
    g, idx = lax.top_k(aff.T, cap)
    return g, idx


COMBINE_CHUNK = 256


def _combine_kernel(src_ref, bnd_ref, ye_hbm, tok_ref, x_ref, mod_ref, g_ref, b_ref, xo_ref,
                    buf_ref, acc_ref, cnt_ref, sem, *, alpha, n_chunks):
    b = pl.program_id(0)
    tb = x_ref.shape[0]
    cs = COMBINE_CHUNK

    def issue(kk):
        slot = kk % 2
        base = kk * cs

        def one(i, carry):
            pltpu.make_async_copy(ye_hbm.at[pl.ds(src_ref[base + i], 1), :],
                                  buf_ref.at[slot, pl.ds(i, 1), :], sem.at[slot]).start()
            return carry
        lax.fori_loop(0, cs, one, 0, unroll=8)
        cnt_ref[0] = kk + 1

    @pl.when(b == 0)
    def _():
        cnt_ref[0] = 0
        cnt_ref[1] = 0
        issue(0)

    lo, hi = bnd_ref[b], bnd_ref[b + 1]
    k0 = lo // cs
    k1 = jnp.where(hi > lo, (hi + cs - 1) // cs, k0)
    acc_ref[...] = jnp.zeros_like(acc_ref)
    row_tok = b * tb + lax.broadcasted_iota(jnp.int32, (tb, cs), 0)

    def chunk(kk, carry):
        slot = kk % 2

        @pl.when(kk >= cnt_ref[1])
        def _():
            pltpu.make_async_copy(buf_ref.at[slot], buf_ref.at[slot], sem.at[slot]).wait()
            cnt_ref[1] = kk + 1

        @pl.when(jnp.logical_and(kk + 1 < n_chunks, kk + 1 >= cnt_ref[0]))
        def _():
            issue(kk + 1)

        toks = tok_ref[:, pl.ds(pl.multiple_of(kk * cs, cs), cs)]
        onehot = jnp.where(row_tok == toks, 1.0, 0.0).astype(BF16)
        acc_ref[...] += jnp.dot(onehot, buf_ref[slot].astype(BF16), preferred_element_type=F32)
        return carry

    lax.fori_loop(k0, k1, chunk, 0)

    mod = mod_ref[...]
    xo_ref[...] = _resid_ln(x_ref[...], acc_ref[...], mod[5:6, :], g_ref[...], b_ref[...], alpha)


def _combine_call(ye, idx, xs1, mods, ln_g, ln_b, rows, n_lat_tiles, alpha):
    ne, cap, d = ye.shape
    tb = ROW_TILE
    cs = COMBINE_CHUNK
    nb = rows // tb
    n_list = ne * cap
    n_pad = -(-n_list // cs) * cs
    tok_flat = idx.reshape(-1)
    order = jnp.argsort(tok_flat).astype(jnp.int32)
    tok_sorted = jnp.take(tok_flat, order)
    bounds = jnp.searchsorted(tok_sorted, jnp.arange(nb + 1, dtype=jnp.int32) * tb).astype(jnp.int32)
    tok_pad = jnp.full((1, n_pad), -1, jnp.int32).at[0, :n_list].set(tok_sorted)
    src_pad = jnp.zeros((n_pad,), jnp.int32).at[:n_list].set(order)
    row = pl.BlockSpec((tb, d), lambda i, *_: (i, 0))
    vec = pl.BlockSpec((1, d), lambda i, *_: (0, 0))
    grid_spec = pltpu.PrefetchScalarGridSpec(
        num_scalar_prefetch=2,
        grid=(nb,),
        in_specs=[pl.BlockSpec(memory_space=pl.ANY),
                  pl.BlockSpec((1, n_pad), lambda i, *_: (0, 0)),
                  row,
                  pl.BlockSpec((None, N_MOD, d),
                               lambda i, *_: (jnp.where(i >= n_lat_tiles, 1, 0), 0, 0)),
                  vec, vec],
        out_specs=row,
        scratch_shapes=[pltpu.VMEM((2, cs, d), F32), pltpu.VMEM((tb, d), F32),
                        pltpu.SMEM((2,), jnp.int32), pltpu.SemaphoreType.DMA((2,))],
    )
    return pl.pallas_call(
        functools.partial(_combine_kernel, alpha=alpha, n_chunks=n_pad // cs),
        grid_spec=grid_spec,
        out_shape=jax.ShapeDtypeStruct((rows, d), F32),
        compiler_params=_cparams("arbitrary"),
        name="moe_combine_ln",
    )(src_pad, bounds, ye.reshape(n_list, d), tok_pad, xs1, mods,
      ln_g.reshape(1, d), ln_b.reshape(1, d))


def _spread_matrix(n_heads, direction):
    r = jnp.arange(4 * n_heads)[:, None] % (2 * n_heads)
    ch = jnp.arange(n_heads * HEAD_DIM)[None, :] // HEAD_DIM
    return (r == ch + direction * n_heads).astype(BF16)


def kernel(x, c, ctx, c_ctx, w_ada, b_ada, w_in, ssm_conv_w, ssm_conv_b, ssm_dt_bias, ssm_a_log,
           ssm_d, ssm_norm_w, w_ssm_out, conv_dw_w, conv_dw_b, conv_ln_g, conv_ln_b, w_conv_out,
           w_o, ln1_g, ln1_b, w_router, w_exp_gate, w_exp_up, w_exp_down, ln2_g, ln2_b):
    batch, n, d = x.shape
    nc = ctx.shape[1]
    depth = w_ada.shape[0]
    assert batch == 1 and n % ROW_TILE == 0 and nc % ROW_TILE == 0 and n % nc == 0
    d_inner = 2 * d
    n_heads = d_inner // HEAD_DIM
    d_bc = GROUPS * STATE
    d_xbc = d_inner + 2 * d_bc
    d_conv = d
    o_xbc = d_inner
    o_dt = o_xbc + d_xbc
    o_glu = o_dt + 2 * n_heads
    alpha = (2 * depth) ** 0.25
    nl, ncc = n // CHUNK, nc // CHUNK
    n_lat_tiles = n // ROW_TILE

    xs = jnp.concatenate([x[0], ctx[0]], axis=0)
    cvec = jnp.zeros((SUBLANES, d), F32).at[0].set(c[0]).at[1].set(c_ctx)
    mods_all = _ada_call(cvec, w_ada, b_ada)
    e_mats = [_spread_matrix(n_heads, dr) for dr in range(2)]

    for i in range(depth):
        last = i == depth - 1
        mods = mods_all[i, :2].reshape(2, N_MOD, d)
        rows_all = n + nc
        rows = n if last else rows_all

        u_in = _modulate_call(xs, mods, rows_all, n_lat_tiles)
        z = _mm_call(u_in, w_in, i, 0, d_inner, rows=rows, name="proj_z")
        xbc = _mm_call(u_in, w_in, i, o_xbc, d_xbc, rows=rows_all, name="proj_xbc",
                       slab_major=True)
        dtraw = _mm_call(u_in, w_in, i, o_dt, 2 * n_heads, rows=rows_all, name="proj_dt")
        glu = _mm_call(u_in, w_in, i, o_glu, 2 * d_conv + 2 * d, rows=rows, name="proj_glu_gate")

        conv_w = jnp.zeros((SUBLANES, d_xbc), F32).at[:SSM_CONV].set(ssm_conv_w[i])
        conv_b = ssm_conv_b[i].reshape(1, d_xbc)
        dt_bias = ssm_dt_bias[i].reshape(1, 2 * n_heads)
        a_row = -jnp.exp(ssm_a_log[i].astype(F32)).reshape(1, 2 * n_heads)
        d_row = jnp.repeat(ssm_d[i], HEAD_DIM).reshape(1, d_inner)
        y = _ssd_call(xbc, dtraw, conv_w, conv_b, dt_bias, a_row, e_mats, d_row,
                      nl=nl, ncc=ncc, d_inner=d_inner)
        a_act = _gnorm_call(y, z, ssm_norm_w[i].reshape(1, d_inner), rows, d_inner)

        hcv = _glu_conv_call(glu, conv_dw_w[i], conv_dw_b[i], n=n, nc=rows - n, d_conv=d_conv)
        b_act = _lnswish_call([hcv], conv_ln_g[i], conv_ln_b[i])

        t1 = _mm_call(a_act, w_ssm_out, i, 0, d, rows=rows, name="ssm_out",
                      epi_fn=lambda acc, gt: _sigmoid(gt) * acc,
                      epi_cols=[2 * d_conv], epi_args=[glu])
        t2 = _mm_call(b_act, w_conv_out, i, 0, d, rows=rows, name="conv_out", out_dtype=BF16,
                      epi_fn=lambda acc, gt, prev: prev + _sigmoid(gt) * acc,
                      epi_cols=[2 * d_conv + d, 0], epi_args=[glu, t1])
        mix = _mm_call(t2, w_o, i, 0, d, rows=rows, name="w_o")

        xs1, aff = _resid_call(xs, mix, mods, ln1_g[i], ln1_b[i], rows, n_lat_tiles, alpha,
                               w_router=w_router[i])

        g_l, idx_l = _route(aff[:n], n)
        gsel, idx = g_l, idx_l
        if not last:
            g_c, idx_c = _route(aff[n:], nc)
            gsel = jnp.concatenate([g_l, g_c], axis=1)
            idx = jnp.concatenate([idx_l, idx_c + n], axis=1)
        ye = _experts_call(idx, xs1, mods, w_exp_gate, w_exp_up, w_exp_down, gsel[..., None], i,
                           idx_l.shape[1])
        xs = _combine_call(ye, idx, xs1, mods, ln2_g[i], ln2_b[i], rows, n_lat_tiles, alpha)

    return xs[:n].reshape(1, n, d)
```

```python
import functools

import jax
import jax.numpy as jnp
from jax import lax
from jax.experimental import pallas as pl
from jax.experimental.pallas import tpu as pltpu

F32 = jnp.float32
BF16 = jnp.bfloat16

HEAD_DIM = 64
GROUPS = 8
STATE = 128
SSM_CONV = 5
CHUNK = 128
CONV_K = 31
GRID_W = 64
N_EXPERTS = 16
EC_CAPACITY_FACTOR = 2
N_MOD = 6
LN_EPS = 1e-5

LANES = 128
SUBLANES = 8
VMEM_LIMIT = 60 * 1024 * 1024
MM_VMEM_BUDGET = 47 * 1024 * 1024
MXU_FLOPS = 1.1e15
MXU_WEIGHT_ROWS = 256
HBM_BYTES_PER_S = 3.3e12
GRID_STEP_S = 0.35e-6

ROW_TILE = 256
HALO = SUBLANES


def _cparams(*sem):
    return pltpu.CompilerParams(dimension_semantics=sem, vmem_limit_bytes=VMEM_LIMIT)


def _sigmoid(v):
    return jax.nn.sigmoid(v)


def _silu(v):
    return v * jax.nn.sigmoid(v)


def _softplus(v):
    return jnp.maximum(v, 0.0) + jnp.log1p(jnp.exp(-jnp.abs(v)))


def _ada_kernel(c_ref, w_ref, b_ref, o_ref):
    a = _silu(c_ref[...])
    o_ref[...] = jnp.dot(a, w_ref[...], preferred_element_type=F32,
                         precision=lax.Precision.HIGHEST) + b_ref[...]


def _ada_call(cvec, w_ada, b_ada):
    depth, d, nm = w_ada.shape
    tn = 1024 if nm % 1024 == 0 else nm
    return pl.pallas_call(
        _ada_kernel,
        grid=(depth, nm // tn),
        in_specs=[
            pl.BlockSpec((SUBLANES, d), lambda l, j: (0, 0)),
            pl.BlockSpec((None, d, tn), lambda l, j: (l, 0, j)),
            pl.BlockSpec((None, 1, tn), lambda l, j: (l, 0, j)),
        ],
        out_specs=pl.BlockSpec((None, SUBLANES, tn), lambda l, j: (l, 0, j)),
        out_shape=jax.ShapeDtypeStruct((depth, SUBLANES, nm), F32),
        compiler_params=_cparams("arbitrary", "arbitrary"),
        name="adaln",
    )(cvec, w_ada, b_ada.reshape(depth, 1, nm))


def _mm_kernel(*refs, n_epi, epi_fn, tm):
    x_ref, w_ref = refs[0], refs[1]
    epi = refs[2:2 + n_epi]
    o_ref = refs[2 + n_epi]
    wb_ref = refs[3 + n_epi]
    i = pl.program_id(1)

    @pl.when(i == 0)
    def _():
        wb_ref[...] = w_ref[...].astype(BF16)

    if x_ref.shape[0] == tm:
        x = x_ref[...]
    else:
        x = x_ref[pl.ds(pl.multiple_of(i * tm, tm), tm), :]
    acc = jnp.dot(x, wb_ref[...], preferred_element_type=F32)
    if epi_fn is not None:
        acc = epi_fn(acc, *[r[...] for r in epi])
    if len(o_ref.shape) == 3:
        perm_ref = refs[4 + n_epi]
        n_grp = CHUNK // SUBLANES
        for jj in range(o_ref.shape[0]):
            perm_ref[...] = acc[:, jj * LANES:(jj + 1) * LANES]
            for c0 in range(0, tm, CHUNK):
                for m in range(n_grp):
                    o_ref[jj, c0 + m * SUBLANES:c0 + (m + 1) * SUBLANES, :] = (
                        perm_ref[pl.ds(c0 + m, SUBLANES, stride=n_grp), :].astype(o_ref.dtype))
    else:
        o_ref[...] = acc.astype(o_ref.dtype)


def _mm_tiles(rows, k, n_out, n_epi, out_bytes):
    best = None
    for tm in range(LANES, rows + 1, LANES):
        if rows % tm:
            continue
        for tn in (1024, 512, 256, 128):
            if n_out % tn:
                continue
            for resident in (False, True):
                x_vmem = rows * k * 2 if resident else 2 * tm * k * 2
                vmem = (x_vmem + 2 * k * tn * 4 + k * tn * 2 + 2 * tm * tn * out_bytes
                        + n_epi * 3 * tm * tn * 4 + tm * tn * 4)
                if vmem > MM_VMEM_BUDGET:
                    continue
                x_reads = 1 if resident else n_out // tn
                t_mxu = 2.0 * rows * k * n_out / MXU_FLOPS * (1.0 + MXU_WEIGHT_ROWS / tm)
                if tn < MXU_WEIGHT_ROWS:
                    t_mxu *= 2.0
                t_hbm = (rows * k * 2.0 * x_reads + k * n_out * 4.0
                         + rows * n_out * (out_bytes + 4.0 * n_epi)) / HBM_BYTES_PER_S
                t = t_mxu + 0.5 * t_hbm + (rows // tm) * (n_out // tn) * GRID_STEP_S
                if best is None or t < best[0]:
                    best = (t, tm, tn, resident)
    assert best is not None
    return best[1:]


def _mm_call(x, w, layer, col_off, n_out, *, rows, name, out_dtype=F32,
             epi_fn=None, epi_cols=(), epi_args=(), slab_major=False):
    k = x.shape[1]
    tm, tn, resident = _mm_tiles(rows, k, n_out, len(epi_args), jnp.dtype(out_dtype).itemsize)
    assert col_off % LANES == 0
    kern = functools.partial(_mm_kernel, n_epi=len(epi_args), epi_fn=epi_fn, tm=tm)
    if resident and rows > tm:
        x_spec = pl.BlockSpec((rows, k), lambda j, i: (0, 0), pipeline_mode=pl.Buffered(1))
    else:
        x_spec = pl.BlockSpec((tm, k), lambda j, i: (i, 0))
    epi_specs = [pl.BlockSpec((tm, tn), functools.partial(
        lambda j, i, off: (i, j + off), off=c // tn)) for c in epi_cols]
    assert all(c % tn == 0 for c in epi_cols)
    if slab_major:
        out_spec = pl.BlockSpec((tn // LANES, tm, LANES), lambda j, i: (j, i, 0))
        out_shape = jax.ShapeDtypeStruct((n_out // LANES, rows, LANES), out_dtype)
    else:
        out_spec = pl.BlockSpec((tm, tn), lambda j, i: (i, j))
        out_shape = jax.ShapeDtypeStruct((rows, n_out), out_dtype)
    return pl.pallas_call(
        kern,
        grid=(n_out // tn, rows // tm),
        in_specs=[x_spec,
                  pl.BlockSpec((pl.Element(k), pl.Element(tn)),
                               lambda j, i: (layer * k, pl.multiple_of(col_off + j * tn, LANES))),
                  *epi_specs],
        out_specs=out_spec,
        out_shape=out_shape,
        scratch_shapes=[pltpu.VMEM((k, tn), BF16)]
        + ([pltpu.VMEM((tm, LANES), F32)] if slab_major else []),
        compiler_params=_cparams("arbitrary", "arbitrary"),
        name=name,
    )(x, w.reshape(-1, w.shape[-1]), *epi_args)


def _modulate_kernel(x_ref, mod_ref, o_ref):
    mod = mod_ref[...]
    o_ref[...] = (x_ref[...] * (1.0 + mod[1:2, :]) + mod[0:1, :]).astype(o_ref.dtype)


def _modulate_call(xs, mods, rows, n_lat_tiles):
    d = xs.shape[1]
    tm = ROW_TILE
    return pl.pallas_call(
        _modulate_kernel,
        grid=(rows // tm,),
        in_specs=[pl.BlockSpec((tm, d), lambda i: (i, 0)),
                  pl.BlockSpec((None, N_MOD, d),
                               lambda i: (jnp.where(i >= n_lat_tiles, 1, 0), 0, 0))],
        out_specs=pl.BlockSpec((tm, d), lambda i: (i, 0)),
        out_shape=jax.ShapeDtypeStruct((rows, d), BF16),
        compiler_params=_cparams("parallel"),
        name="modulate",
    )(xs, mods)


def _ssd_chunk_of_step(s, direction, nl, ncc):
    if direction == 0:
        return jnp.where(s < ncc, nl + s, s - ncc)
    return jnp.where(s < ncc, nl + ncc - 1 - s, nl - 1 - (s - ncc))


def _ssd_kernel(*refs, direction, nl, ncc, d_inner):
    if direction == 0:
        (xm_ref, xpa_ref, xpb_ref, xna_ref, xnb_ref, dt_ref, cw_ref, cb_ref, dtb_ref, arow_ref,
         e_ref, dsk_ref, y_ref, xc_ref, state_ref, ext_ref) = refs
        yprev_ref = None
    else:
        (xc_ref, dt_ref, dtb_ref, arow_ref, e_ref, yprev_ref, y_ref, state_ref, ext_ref,
         ys_ref) = refs
    q = CHUNK
    d_bc = GROUPS * STATE
    d_xbc = d_inner + 2 * d_bc
    gw = d_inner // GROUPS
    hpg = gw // HEAD_DIM
    n_heads = d_inner // HEAD_DIM
    n_grp = q // SUBLANES

    s = pl.program_id(0)
    c = _ssd_chunk_of_step(s, direction, nl, ncc)

    @pl.when(s == 0)
    def _():
        state_ref[...] = jnp.zeros_like(state_ref)

    def slabs(ref3, rows):
        return jnp.concatenate([ref3[j, rows, :] for j in range(ref3.shape[0])], axis=1)

    if direction == 0:
        pad = SSM_CONV // 2
        seg_first = jnp.logical_or(c == 0, c == nl)
        seg_last = jnp.logical_or(c == nl - 1, c == nl + ncc - 1)
        last, first = slice(SUBLANES - 1, SUBLANES), slice(0, 1)
        prev = [jnp.where(seg_first, 0.0, slabs(r, last)) for r in (xpa_ref, xpb_ref)]
        nxt = [jnp.where(seg_last, 0.0, slabs(r, first)) for r in (xna_ref, xnb_ref)]
        sub = lax.broadcasted_iota(jnp.int32, (SUBLANES, d_xbc), 0)
        for m in range(n_grp):
            grp = slabs(xm_ref, slice(m * SUBLANES, (m + 1) * SUBLANES))
            ext_ref[(pad + m) * SUBLANES:(pad + m + 1) * SUBLANES, :] = grp
            if m >= n_grp - pad:
                j = m - (n_grp - pad)
                halo = prev[j]
                ext_ref[j * SUBLANES:(j + 1) * SUBLANES, :] = jnp.where(
                    sub == 0, halo, pltpu.roll(grp, 1, axis=0))
            if m < pad:
                halo = nxt[m]
                ext_ref[(pad + n_grp + m) * SUBLANES:(pad + n_grp + m + 1) * SUBLANES, :] = (
                    jnp.where(sub == SUBLANES - 1, halo, pltpu.roll(grp, SUBLANES - 1, axis=0)))
        ct = 512
        for j in range(d_xbc // ct):
            cs = slice(j * ct, (j + 1) * ct)
            acc = jnp.broadcast_to(cb_ref[:, cs], (q, ct))
            for k in range(SSM_CONV):
                acc = acc + cw_ref[k:k + 1, cs] * ext_ref[k * SUBLANES:k * SUBLANES + q, cs]
            xc_ref[:, cs] = _silu(acc)

    dt_raw = jnp.concatenate([dt_ref[pl.ds(m, SUBLANES, stride=n_grp), :] for m in range(n_grp)],
                             axis=0)
    dt = _softplus(dt_raw + dtb_ref[...])
    a = dt * arow_ref[...]
    ri = lax.broadcasted_iota(jnp.int32, (q, q), 0)
    ci = lax.broadcasted_iota(jnp.int32, (q, q), 1)
    tok_i = (ri % SUBLANES) * n_grp + ri // SUBLANES
    tok_j = (ci % SUBLANES) * n_grp + ci // SUBLANES
    causal = (tok_i >= tok_j) if direction == 0 else (tok_i <= tok_j)
    cum = jnp.dot(causal.astype(F32), a, preferred_element_type=F32,
                  precision=lax.Precision.HIGHEST)
    tot = cum[q - 1:q, :] if direction == 0 else cum[0:1, :]
    e_in = jnp.exp(cum)
    w_st = dt * jnp.exp(tot - cum)
    cum_t = cum.T
    dt_t = dt.T

    stack = jnp.concatenate([w_st, e_in], axis=0)
    hi = stack.astype(BF16)
    lo = (stack - hi.astype(F32)).astype(BF16)
    spread = jnp.dot(jnp.concatenate([hi, lo], axis=1), e_ref[...],
                     preferred_element_type=F32)
    ext_ref[0:2 * q, 0:d_inner] = spread
    dec_row = q - 1 if direction == 0 else 0

    lane = lax.broadcasted_iota(jnp.int32, (q, LANES), 1)
    first_head = lane < HEAD_DIM
    hoff = direction * n_heads

    for g in range(GROUPS):
        bg = xc_ref[:, d_inner + g * STATE:d_inner + (g + 1) * STATE]
        cg = xc_ref[:, d_inner + d_bc + g * STATE:d_inner + d_bc + (g + 1) * STATE]
        bgb = bg.astype(BF16)
        cgb = cg.astype(BF16)
        scores = lax.dot_general(cgb, bgb, (((1,), (1,)), ((), ())),
                                 preferred_element_type=F32)
        gs = slice(g * gw, (g + 1) * gw)
        st = state_ref[:, gs]
        y_g = jnp.dot(cgb, st.astype(BF16), preferred_element_type=F32) * ext_ref[q:2 * q, gs]
        xg = xc_ref[:, gs]
        xw = (xg * ext_ref[0:q, gs]).astype(BF16)
        s_new = jnp.dot(bg.T.astype(BF16), xw, preferred_element_type=F32)
        state_ref[:, gs] = st * ext_ref[q + dec_row:q + dec_row + 1, gs] + s_new
        parts = []
        for p in range(hpg // 2):
            slab = xg[:, p * LANES:(p + 1) * LANES]
            y_p = None
            for half in range(2):
                h = hoff + g * hpg + 2 * p + half
                seg = cum[:, h:h + 1] - cum_t[h:h + 1, :]
                m = scores * jnp.exp(jnp.where(causal, seg, -jnp.inf)) * dt_t[h:h + 1, :]
                keep = first_head if half == 0 else jnp.logical_not(first_head)
                xh = jnp.where(keep, slab, 0.0).astype(BF16)
                t = jnp.dot(m.astype(BF16), xh, preferred_element_type=F32)
                y_p = t if y_p is None else y_p + t
            parts.append(y_p)
        y_g = y_g + jnp.concatenate(parts, axis=1)
        if direction == 0:
            y_ref[:, gs] = y_g + dsk_ref[:, gs] * xg
        else:
            y_g = y_g + yprev_ref[:, gs]
            for jl in range(gw // LANES):
                ys_ref[...] = y_g[:, jl * LANES:(jl + 1) * LANES]
                jg = g * (gw // LANES) + jl
                for a in range(n_grp):
                    start = (a % 2) * (q // 2) + a // 2
                    y_ref[jg, a * SUBLANES:(a + 1) * SUBLANES, :] = (
                        ys_ref[pl.ds(start, SUBLANES, stride=SUBLANES), :])


def _ssd_call(xbc, dtraw, conv_w, conv_b, dt_bias, a_row, e_mats, d_row, *, nl, ncc, d_inner):
    rows = (nl + ncc) * CHUNK
    d_xbc = d_inner + 2 * GROUPS * STATE
    nh2 = dtraw.shape[1]
    q = CHUNK
    qh = q // HALO
    nchunks = nl + ncc
    full = lambda shape: pl.BlockSpec(shape, lambda s: (0,) * len(shape))
    scratch = [pltpu.VMEM((STATE, d_inner), F32), pltpu.VMEM((2 * q, d_xbc), F32)]

    def chunk_spec(width, direction):
        cmap = functools.partial(_ssd_chunk_of_step, direction=direction, nl=nl, ncc=ncc)
        return pl.BlockSpec((q, width), lambda s: (cmap(s), 0))

    cmap = functools.partial(_ssd_chunk_of_step, direction=0, nl=nl, ncc=ncc)
    cmap_b = functools.partial(_ssd_chunk_of_step, direction=1, nl=nl, ncc=ncc)
    ns_x, ns_y = d_xbc // LANES, d_inner // LANES
    main_spec = pl.BlockSpec((ns_x, q, LANES), lambda s: (0, cmap(s), 0))
    pad = SSM_CONV // 2

    def halo_spec(offset):
        return pl.BlockSpec((ns_x, HALO, LANES), lambda s: (
            0, jnp.clip(cmap(s) * qh + offset, 0, nchunks * qh - 1), 0))

    halo_specs = [halo_spec(o) for o in (-pad, -pad + 1, qh, qh + 1)]
    assert pad == 2
    y_fwd, xc = pl.pallas_call(
        functools.partial(_ssd_kernel, direction=0, nl=nl, ncc=ncc, d_inner=d_inner),
        grid=(nchunks,),
        in_specs=[main_spec, *halo_specs, chunk_spec(nh2, 0),
                  full((SUBLANES, d_xbc)), full((1, d_xbc)), full((1, nh2)), full((1, nh2)),
                  full(e_mats[0].shape), full((1, d_inner))],
        out_specs=[chunk_spec(d_inner, 0), chunk_spec(d_xbc, 0)],
        out_shape=[jax.ShapeDtypeStruct((rows, d_inner), F32),
                   jax.ShapeDtypeStruct((rows, d_xbc), F32)],
        scratch_shapes=scratch,
        compiler_params=_cparams("arbitrary"),
        name="ssd_fwd",
    )(xbc, xbc, xbc, xbc, xbc, dtraw, conv_w, conv_b, dt_bias, a_row, e_mats[0], d_row)
    return pl.pallas_call(
        functools.partial(_ssd_kernel, direction=1, nl=nl, ncc=ncc, d_inner=d_inner),
        grid=(nchunks,),
        in_specs=[chunk_spec(d_xbc, 1), chunk_spec(nh2, 1), full((1, nh2)), full((1, nh2)),
                  full(e_mats[1].shape), chunk_spec(d_inner, 1)],
        out_specs=pl.BlockSpec((ns_y, q, LANES), lambda s: (0, cmap_b(s), 0)),
        out_shape=jax.ShapeDtypeStruct((ns_y, rows, LANES), F32),
        scratch_shapes=scratch + [pltpu.VMEM((q, LANES), F32)],
        compiler_params=_cparams("arbitrary"),
        name="ssd_bwd",
    )(xc, dtraw, dt_bias, a_row, e_mats[1], y_fwd)


def _gnorm_kernel(y_ref, z_ref, w_ref, o_ref, *, d_inner):
    gw = d_inner // GROUPS
    spg = gw // LANES
    for g in range(GROUPS):
        gs = slice(g * gw, (g + 1) * gw)
        y = jnp.concatenate([y_ref[g * spg + j] for j in range(spg)], axis=1)
        h = y * _silu(z_ref[:, gs])
        ms = jnp.mean(h * h, axis=-1, keepdims=True)
        o_ref[:, gs] = (h * lax.rsqrt(ms + LN_EPS) * w_ref[:, gs]).astype(o_ref.dtype)


def _gnorm_call(y, zx, norm_w, rows, d_inner):
    tm = ROW_TILE
    return pl.pallas_call(
        functools.partial(_gnorm_kernel, d_inner=d_inner),
        grid=(rows // tm,),
        in_specs=[pl.BlockSpec((d_inner // LANES, tm, LANES), lambda i: (0, i, 0)),
                  pl.BlockSpec((tm, d_inner), lambda i: (i, 0)),
                  pl.BlockSpec((1, d_inner), lambda i: (0, 0))],
        out_specs=pl.BlockSpec((tm, d_inner), lambda i: (i, 0)),
        out_shape=jax.ShapeDtypeStruct((rows, d_inner), BF16),
        compiler_params=_cparams("parallel"),
        name="gated_rmsnorm",
    )(y, zx, norm_w)


CONV_GAP = 2 * SUBLANES


def _conv_runs(ga_ref, gb_ref, w_ref, b_ref, o_ref, pad_ref, *, row0, seq, n_seq):
    half = CONV_K // 2
    gap = CONV_GAP
    rb = pad_ref.shape[0]
    pad_ref[:, 0:gap, :] = jnp.zeros((rb, gap, LANES), F32)
    pad_ref[:, gap + seq:gap + seq + gap, :] = jnp.zeros((rb, gap, LANES), F32)
    bias = b_ref[...]

    def body(it, carry):
        r0 = pl.multiple_of(row0 + it * (rb * seq), SUBLANES)
        ga = ga_ref[pl.ds(r0, rb * seq), :]
        gb = gb_ref[pl.ds(r0, rb * seq), :]
        pad_ref[:, gap:gap + seq, :] = (ga * _sigmoid(gb)).reshape(rb, seq, LANES)
        acc = jnp.broadcast_to(bias.reshape(1, 1, LANES), (rb, seq, LANES))
        for k in range(CONV_K):
            o = gap - half + k
            acc = acc + w_ref[k:k + 1, :].reshape(1, 1, LANES) * pad_ref[:, o:o + seq, :]
        o_ref[pl.ds(r0, rb * seq), :] = acc.reshape(rb * seq, LANES)
        return carry

    lax.fori_loop(0, n_seq // rb, body, 0)


def _conv_stride(ga_ref, gb_ref, w_ref, b_ref, o_ref, pad_ref, *, n, stride):
    half = CONV_K // 2
    halo = half * stride
    pad_ref[0:halo, :] = jnp.zeros((halo, LANES), F32)
    pad_ref[halo + n:halo + n + halo, :] = jnp.zeros((halo, LANES), F32)
    tb = 256
    bias = b_ref[...]

    def fill(it, carry):
        r0 = pl.multiple_of(it * tb, tb)
        pad_ref[pl.ds(halo + r0, tb), :] = ga_ref[pl.ds(r0, tb), :] * _sigmoid(gb_ref[pl.ds(r0, tb), :])
        return carry

    lax.fori_loop(0, n // tb, fill, 0)

    def body(it, carry):
        r0 = pl.multiple_of(it * tb, tb)
        acc = jnp.broadcast_to(bias, (tb, LANES))
        for k in range(CONV_K):
            acc = acc + w_ref[k:k + 1, :] * pad_ref[pl.ds(r0 + k * stride, tb), :]
        o_ref[pl.ds(r0, tb), :] = acc
        return carry

    lax.fori_loop(0, n // tb, body, 0)


def _glu_conv_kernel(ga_ref, gb_ref, w_ref, b_ref, o_ref, pad_rows, pad_cols, pad_ctx,
                     *, n, nc, n_row_tiles):
    c = pl.program_id(0)

    @pl.when(c < n_row_tiles)
    def _():
        _conv_runs(ga_ref, gb_ref, w_ref, b_ref, o_ref, pad_rows, row0=0, seq=GRID_W,
                   n_seq=n // GRID_W)

    @pl.when(c >= n_row_tiles)
    def _():
        _conv_stride(ga_ref, gb_ref, w_ref, b_ref, o_ref, pad_cols, n=n, stride=GRID_W)

    if nc:
        _conv_runs(ga_ref, gb_ref, w_ref, b_ref, o_ref, pad_ctx, row0=n, seq=nc, n_seq=1)


def _glu_conv_call(glu, conv_w, conv_b, *, n, nc, d_conv):
    rows = n + nc
    half = d_conv // 2
    assert half % LANES == 0 and n % (4 * GRID_W) == 0
    gblk = d_conv // LANES
    wpad = jnp.zeros((32, d_conv), F32).at[:CONV_K].set(conv_w)
    bias = conv_b.reshape(1, d_conv)
    kern = functools.partial(_glu_conv_kernel, n=n, nc=nc, n_row_tiles=half // LANES)
    return pl.pallas_call(
        kern,
        grid=(d_conv // LANES,),
        in_specs=[pl.BlockSpec((rows, LANES), lambda c: (0, c)),
                  pl.BlockSpec((rows, LANES), lambda c: (0, gblk + c)),
                  pl.BlockSpec((32, LANES), lambda c: (0, c)),
                  pl.BlockSpec((1, LANES), lambda c: (0, c))],
        out_specs=pl.BlockSpec((rows, LANES), lambda c: (0, c)),
        out_shape=jax.ShapeDtypeStruct((rows, d_conv), F32),
        scratch_shapes=[pltpu.VMEM((4, GRID_W + 2 * CONV_GAP, LANES), F32),
                        pltpu.VMEM((n + 2 * (CONV_K // 2) * GRID_W, LANES), F32),
                        pltpu.VMEM((1, max(nc, SUBLANES) + 2 * CONV_GAP, LANES), F32)],
        compiler_params=_cparams("parallel"),
        name="glu_conv",
    )(glu, glu, wpad, bias)


def _lnswish_kernel(*refs, n_in):
    ins = refs[:n_in]
    g_ref, b_ref, o_ref = refs[n_in:]
    x = jnp.concatenate([r[...] for r in ins], axis=1) if n_in > 1 else ins[0][...]
    mu = jnp.mean(x, axis=-1, keepdims=True)
    xc = x - mu
    var = jnp.mean(xc * xc, axis=-1, keepdims=True)
    y = xc * lax.rsqrt(var + LN_EPS) * g_ref[...] + b_ref[...]
    o_ref[...] = _silu(y).astype(o_ref.dtype)


def _lnswish_call(parts, ln_g, ln_b):
    rows = parts[0].shape[0]
    d = sum(p.shape[1] for p in parts)
    tm = ROW_TILE
    return pl.pallas_call(
        functools.partial(_lnswish_kernel, n_in=len(parts)),
        grid=(rows // tm,),
        in_specs=[*[pl.BlockSpec((tm, p.shape[1]), lambda i: (i, 0)) for p in parts],
                  pl.BlockSpec((1, d), lambda i: (0, 0)),
                  pl.BlockSpec((1, d), lambda i: (0, 0))],
        out_specs=pl.BlockSpec((tm, d), lambda i: (i, 0)),
        out_shape=jax.ShapeDtypeStruct((rows, d), BF16),
        compiler_params=_cparams("parallel"),
        name="conv_ln_swish",
    )(*parts, ln_g.reshape(1, d), ln_b.reshape(1, d))


def _resid_ln(x, t, gate, g, b, alpha):
    v = alpha * x + gate * t
    mu = jnp.mean(v, axis=-1, keepdims=True)
    vc = v - mu
    var = jnp.mean(vc * vc, axis=-1, keepdims=True)
    return vc * lax.rsqrt(var + LN_EPS) * g + b


def _resid_ln_router_kernel(x_ref, t_ref, mod_ref, g_ref, b_ref, wr_ref, xo_ref, aff_ref,
                            *, alpha):
    mod = mod_ref[...]
    xn = _resid_ln(x_ref[...], t_ref[...], mod[2:3, :], g_ref[...], b_ref[...], alpha)
    xo_ref[...] = xn
    u = xn * (1.0 + mod[4:5, :]) + mod[3:4, :]
    logits = jnp.dot(u, wr_ref[...], preferred_element_type=F32, precision=lax.Precision.HIGHEST)
    mx = jnp.max(logits, axis=-1, keepdims=True)
    ex = jnp.exp(logits - mx)
    aff_ref[...] = ex / jnp.sum(ex, axis=-1, keepdims=True)


def _resid_ln_kernel(x_ref, t_ref, mod_ref, g_ref, b_ref, xo_ref, *, alpha):
    mod = mod_ref[...]
    xo_ref[...] = _resid_ln(x_ref[...], t_ref[...], mod[5:6, :], g_ref[...], b_ref[...], alpha)


def _resid_call(x, t, mods, ln_g, ln_b, rows, n_lat_tiles, alpha, w_router=None):
    d = x.shape[1]
    tm = ROW_TILE
    row = pl.BlockSpec((tm, d), lambda i: (i, 0))
    vec = pl.BlockSpec((1, d), lambda i: (0, 0))
    mod_spec = pl.BlockSpec((None, N_MOD, d), lambda i: (jnp.where(i >= n_lat_tiles, 1, 0), 0, 0))
    if w_router is None:
        return pl.pallas_call(
            functools.partial(_resid_ln_kernel, alpha=alpha),
            grid=(rows // tm,),
            in_specs=[row, row, mod_spec, vec, vec],
            out_specs=row,
            out_shape=jax.ShapeDtypeStruct((rows, d), F32),
            compiler_params=_cparams("parallel"),
            name="resid_ln",
        )(x, t, mods, ln_g.reshape(1, d), ln_b.reshape(1, d))
    ne = w_router.shape[1]
    return pl.pallas_call(
        functools.partial(_resid_ln_router_kernel, alpha=alpha),
        grid=(rows // tm,),
        in_specs=[row, row, mod_spec, vec, vec, pl.BlockSpec((d, ne), lambda i: (0, 0))],
        out_specs=[row, pl.BlockSpec((tm, ne), lambda i: (i, 0))],
        out_shape=[jax.ShapeDtypeStruct((rows, d), F32),
                   jax.ShapeDtypeStruct((rows, ne), F32)],
        compiler_params=_cparams("parallel"),
        name="resid_ln_router",
    )(x, t, mods, ln_g.reshape(1, d), ln_b.reshape(1, d), w_router)


def _experts_kernel(idx_ref, x_hbm, mod_ref, wg_ref, wu_ref, wd_ref, g_ref, o_ref,
                    rows_ref, xe_ref, sem, *, rc, cap_lat, nf):
    e = pl.program_id(0)
    f = pl.program_id(1)
    cap = xe_ref.shape[0]
    ne = pl.num_programs(0)
    per = _gather_rows_per_step(cap, nf)
    assert nf * per - cap <= rows_ref.shape[0] - cap

    def issue_row(ee, k):
        real = k < cap
        t = idx_ref[ee, jnp.where(real, k, cap - 1)]
        pltpu.make_async_copy(x_hbm.at[pl.ds(t, 1), :], rows_ref.at[pl.ds(k, 1), :], sem).start()

    def wait_gather():
        full = rows_ref.at[pl.ds(0, nf * per), :]
        pltpu.make_async_copy(full, full, sem).wait()

    @pl.when(jnp.logical_and(e == 0, f == 0))
    def _():
        def issue(k, carry):
            issue_row(0, k)
            return carry
        lax.fori_loop(0, nf * per, issue, 0, unroll=8)

    @pl.when(f == 0)
    def _():
        wait_gather()
        mod = mod_ref[...]
        for lo, hi, m in ((0, cap_lat, 0), (cap_lat, cap, 1)):
            if hi > lo:
                sh, sc = mod[m, 3:4, :], mod[m, 4:5, :]
                xe_ref[lo:hi, :] = (rows_ref[lo:hi, :] * (1.0 + sc) + sh).astype(BF16)
        o_ref[...] = jnp.zeros_like(o_ref)

    e_next = jnp.where(e + 1 < ne, e + 1, 0)
    for r in range(per):
        issue_row(e_next, f * per + r)

    wg = wg_ref[...].astype(BF16)
    wu = wu_ref[...].astype(BF16)
    wd = wd_ref[...].astype(BF16)
    for r0 in range(0, cap, rc):
        xe = xe_ref[r0:r0 + rc, :]
        hg = jnp.dot(xe, wg, preferred_element_type=F32)
        hu = jnp.dot(xe, wu, preferred_element_type=F32)
        h = (_silu(hg) * hu).astype(BF16)
        o_ref[r0:r0 + rc, :] += jnp.dot(h, wd, preferred_element_type=F32)

    @pl.when(f == nf - 1)
    def _():
        o_ref[...] = o_ref[...] * g_ref[...]

        @pl.when(e == ne - 1)
        def _():
            wait_gather()


def _gather_rows_per_step(cap, nf):
    per = -(-cap // nf)
    while (nf * per) % SUBLANES:
        per += 1
    return per


def _row_chunk(cap, limit=384, align=16):
    best = None
    for rc in range(align, min(cap, limit) + 1, align):
        if cap % rc == 0:
            best = rc
    assert best is not None, cap
    return best


def _experts_call(idx, xs, mods, w_gate, w_up, w_down, gsel, layer, cap_lat):
    ne, cap = idx.shape
    d = xs.shape[1]
    dexp = w_gate.shape[3]
    tf = 512 if dexp % 512 == 0 else dexp
    nf = dexp // tf
    grid_spec = pltpu.PrefetchScalarGridSpec(
        num_scalar_prefetch=1,
        grid=(ne, dexp // tf),
        in_specs=[pl.BlockSpec(memory_space=pl.ANY),
                  pl.BlockSpec((2, N_MOD, d), lambda e, f, idx: (0, 0, 0)),
                  pl.BlockSpec((None, None, d, tf), lambda e, f, idx: (layer, e, 0, f)),
                  pl.BlockSpec((None, None, d, tf), lambda e, f, idx: (layer, e, 0, f)),
                  pl.BlockSpec((None, None, tf, d), lambda e, f, idx: (layer, e, f, 0)),
                  pl.BlockSpec((None, cap, 1), lambda e, f, idx: (e, 0, 0))],
        out_specs=pl.BlockSpec((None, cap, d), lambda e, f, idx: (e, 0, 0),
                               pipeline_mode=pl.Buffered(1)),
        scratch_shapes=[pltpu.VMEM((nf * _gather_rows_per_step(cap, nf), d), F32),
                        pltpu.VMEM((cap, d), BF16), pltpu.SemaphoreType.DMA(())],
    )
    return pl.pallas_call(
        functools.partial(_experts_kernel, rc=_row_chunk(cap), cap_lat=cap_lat, nf=nf),
        grid_spec=grid_spec,
        out_shape=jax.ShapeDtypeStruct((ne, cap, d), F32),
        compiler_params=_cparams("arbitrary", "arbitrary"),
        name="experts",
    )(idx, xs, mods, w_gate, w_up, w_down, gsel)


def _route(aff, n_tok):
    cap = max(1, EC_CAPACITY_FACTOR * n_tok // N_EXPERTS)
    g, idx = lax.top_k(aff.T, cap)
    return g, idx


COMBINE_CHUNK = 256


def _combine_kernel(src_ref, bnd_ref, ye_hbm, tok_ref, x_ref, mod_ref, g_ref, b_ref, xo_ref,
                    buf_ref, acc_ref, cnt_ref, sem, *, alpha, n_chunks):
    b = pl.program_id(0)
    tb = x_ref.shape[0]
    cs = COMBINE_CHUNK

    def issue(kk):
        slot = kk % 2
        base = kk * cs

        def one(i, carry):
            pltpu.make_async_copy(ye_hbm.at[pl.ds(src_ref[base + i], 1), :],
                                  buf_ref.at[slot, pl.ds(i, 1), :], sem.at[slot]).start()
            return carry
        lax.fori_loop(0, cs, one, 0, unroll=8)
        cnt_ref[0] = kk + 1

    @pl.when(b == 0)
    def _():
        cnt_ref[0] = 0
        cnt_ref[1] = 0
        issue(0)

    lo, hi = bnd_ref[b], bnd_ref[b + 1]
    k0 = lo // cs
    k1 = jnp.where(hi > lo, (hi + cs - 1) // cs, k0)
    acc_ref[...] = jnp.zeros_like(acc_ref)
    row_tok = b * tb + lax.broadcasted_iota(jnp.int32, (tb, cs), 0)

    def chunk(kk, carry):
        slot = kk % 2

        @pl.when(kk >= cnt_ref[1])
        def _():
            pltpu.make_async_copy(buf_ref.at[slot], buf_ref.at[slot], sem.at[slot]).wait()
            cnt_ref[1] = kk + 1

        @pl.when(jnp.logical_and(kk + 1 < n_chunks, kk + 1 >= cnt_ref[0]))
        def _():
            issue(kk + 1)

        toks = tok_ref[:, pl.ds(pl.multiple_of(kk * cs, cs), cs)]
        onehot = jnp.where(row_tok == toks, 1.0, 0.0).astype(BF16)
        acc_ref[...] += jnp.dot(onehot, buf_ref[slot].astype(BF16), preferred_element_type=F32)
        return carry

    lax.fori_loop(k0, k1, chunk, 0)

    mod = mod_ref[...]
    xo_ref[...] = _resid_ln(x_ref[...], acc_ref[...], mod[5:6, :], g_ref[...], b_ref[...], alpha)


def _combine_call(ye, idx, xs1, mods, ln_g, ln_b, rows, n_lat_tiles, alpha):
    ne, cap, d = ye.shape
    tb = ROW_TILE
    cs = COMBINE_CHUNK
    nb = rows // tb
    n_list = ne * cap
    n_pad = -(-n_list // cs) * cs
    tok_flat = idx.reshape(-1)
    order = jnp.argsort(tok_flat).astype(jnp.int32)
    tok_sorted = jnp.take(tok_flat, order)
    bounds = jnp.searchsorted(tok_sorted, jnp.arange(nb + 1, dtype=jnp.int32) * tb).astype(jnp.int32)
    tok_pad = jnp.full((1, n_pad), -1, jnp.int32).at[0, :n_list].set(tok_sorted)
    src_pad = jnp.zeros((n_pad,), jnp.int32).at[:n_list].set(order)
    row = pl.BlockSpec((tb, d), lambda i, *_: (i, 0))
    vec = pl.BlockSpec((1, d), lambda i, *_: (0, 0))
    grid_spec = pltpu.PrefetchScalarGridSpec(
        num_scalar_prefetch=2,
        grid=(nb,),
        in_specs=[pl.BlockSpec(memory_space=pl.ANY),
                  pl.BlockSpec((1, n_pad), lambda i, *_: (0, 0)),
                  row,
                  pl.BlockSpec((None, N_MOD, d),
                               lambda i, *_: (jnp.where(i >= n_lat_tiles, 1, 0), 0, 0)),
                  vec, vec],
        out_specs=row,
        scratch_shapes=[pltpu.VMEM((2, cs, d), F32), pltpu.VMEM((tb, d), F32),
                        pltpu.SMEM((2,), jnp.int32), pltpu.SemaphoreType.DMA((2,))],
    )
    return pl.pallas_call(
        functools.partial(_combine_kernel, alpha=alpha, n_chunks=n_pad // cs),
        grid_spec=grid_spec,
        out_shape=jax.ShapeDtypeStruct((rows, d), F32),
        compiler_params=_cparams("arbitrary"),
        name="moe_combine_ln",
    )(src_pad, bounds, ye.reshape(n_list, d), tok_pad, xs1, mods,
      ln_g.reshape(1, d), ln_b.reshape(1, d))


def _spread_matrix(n_heads, direction):
    r = jnp.arange(4 * n_heads)[:, None] % (2 * n_heads)
    ch = jnp.arange(n_heads * HEAD_DIM)[None, :] // HEAD_DIM
    return (r == ch + direction * n_heads).astype(BF16)


def kernel(x, c, ctx, c_ctx, w_ada, b_ada, w_in, ssm_conv_w, ssm_conv_b, ssm_dt_bias, ssm_a_log,
           ssm_d, ssm_norm_w, w_ssm_out, conv_dw_w, conv_dw_b, conv_ln_g, conv_ln_b, w_conv_out,
           w_o, ln1_g, ln1_b, w_router, w_exp_gate, w_exp_up, w_exp_down, ln2_g, ln2_b):
    batch, n, d = x.shape
    nc = ctx.shape[1]
    depth = w_ada.shape[0]
    assert batch == 1 and n % ROW_TILE == 0 and nc % ROW_TILE == 0 and n % nc == 0
    d_inner = 2 * d
    n_heads = d_inner // HEAD_DIM
    d_bc = GROUPS * STATE
    d_xbc = d_inner + 2 * d_bc
    d_conv = d
    o_xbc = d_inner
    o_dt = o_xbc + d_xbc
    o_glu = o_dt + 2 * n_heads
    alpha = (2 * depth) ** 0.25
    nl, ncc = n // CHUNK, nc // CHUNK
    n_lat_tiles = n // ROW_TILE

    xs = jnp.concatenate([x[0], ctx[0]], axis=0)
    cvec = jnp.zeros((SUBLANES, d), F32).at[0].set(c[0]).at[1].set(c_ctx)
    mods_all = _ada_call(cvec, w_ada, b_ada)
    e_mats = [_spread_matrix(n_heads, dr) for dr in range(2)]

    for i in range(depth):
        last = i == depth - 1
        mods = mods_all[i, :2].reshape(2, N_MOD, d)
        rows_all = n + nc
        rows = n if last else rows_all

        u_in = _modulate_call(xs, mods, rows_all, n_lat_tiles)
        z = _mm_call(u_in, w_in, i, 0, d_inner, rows=rows, name="proj_z")
        xbc = _mm_call(u_in, w_in, i, o_xbc, d_xbc, rows=rows_all, name="proj_xbc",
                       slab_major=True)
        dtraw = _mm_call(u_in, w_in, i, o_dt, 2 * n_heads, rows=rows_all, name="proj_dt")
        glu = _mm_call(u_in, w_in, i, o_glu, 2 * d_conv + 2 * d, rows=rows, name="proj_glu_gate")

        conv_w = jnp.zeros((SUBLANES, d_xbc), F32).at[:SSM_CONV].set(ssm_conv_w[i])
        conv_b = ssm_conv_b[i].reshape(1, d_xbc)
        dt_bias = ssm_dt_bias[i].reshape(1, 2 * n_heads)
        a_row = -jnp.exp(ssm_a_log[i].astype(F32)).reshape(1, 2 * n_heads)
        d_row = jnp.repeat(ssm_d[i], HEAD_DIM).reshape(1, d_inner)
        y = _ssd_call(xbc, dtraw, conv_w, conv_b, dt_bias, a_row, e_mats, d_row,
                      nl=nl, ncc=ncc, d_inner=d_inner)
        a_act = _gnorm_call(y, z, ssm_norm_w[i].reshape(1, d_inner), rows, d_inner)

        hcv = _glu_conv_call(glu, conv_dw_w[i], conv_dw_b[i], n=n, nc=rows - n, d_conv=d_conv)
        b_act = _lnswish_call([hcv], conv_ln_g[i], conv_ln_b[i])

        t1 = _mm_call(a_act, w_ssm_out, i, 0, d, rows=rows, name="ssm_out",
                      epi_fn=lambda acc, gt: _sigmoid(gt) * acc,
                      epi_cols=[2 * d_conv], epi_args=[glu])
        t2 = _mm_call(b_act, w_conv_out, i, 0, d, rows=rows, name="conv_out", out_dtype=BF16,
                      epi_fn=lambda acc, gt, prev: prev + _sigmoid(gt) * acc,
                      epi_cols=[2 * d_conv + d, 0], epi_args=[glu, t1])
        mix = _mm_call(t2, w_o, i, 0, d, rows=rows, name="w_o")

        xs1, aff = _resid_call(xs, mix, mods, ln1_g[i], ln1_b[i], rows, n_lat_tiles, alpha,
                               w_router=w_router[i])

        g_l, idx_l = _route(aff[:n], n)
        gsel, idx = g_l, idx_l
        if not last:
            g_c, idx_c = _route(aff[n:], nc)
            gsel = jnp.concatenate([g_l, g_c], axis=1)
            idx = jnp.concatenate([idx_l, idx_c + n], axis=1)
        ye = _experts_call(idx, xs1, mods, w_exp_gate, w_exp_up, w_exp_down, gsel[..., None], i,
                           idx_l.shape[1])
        xs = _combine_call(ye, idx, xs1, mods, ln2_g[i], ln2_b[i], rows, n_lat_tiles, alpha)

    return xs[:n].reshape(1, n, d)
```

```python
import functools

import jax
import jax.numpy as jnp
from jax import lax
from jax.experimental import pallas as pl
from jax.experimental.pallas import tpu as pltpu

F32 = jnp.float32
BF16 = jnp.bfloat16

HEAD_DIM = 64
GROUPS = 8
STATE = 128
SSM_CONV = 5
CHUNK = 128
CONV_K = 31
GRID_W = 64
N_EXPERTS = 16
EC_CAPACITY_FACTOR = 2
N_MOD = 6
LN_EPS = 1e-5

LANES = 128
SUBLANES = 8
VMEM_LIMIT = 60 * 1024 * 1024
MM_VMEM_BUDGET = 47 * 1024 * 1024
MXU_FLOPS = 1.1e15
MXU_WEIGHT_ROWS = 256
HBM_BYTES_PER_S = 3.3e12
GRID_STEP_S = 0.35e-6

ROW_TILE = 256
HALO = SUBLANES


def _cparams(*sem):
    return pltpu.CompilerParams(dimension_semantics=sem, vmem_limit_bytes=VMEM_LIMIT)


def _sigmoid(v):
    return jax.nn.sigmoid(v)


def _silu(v):
    return v * jax.nn.sigmoid(v)


def _softplus(v):
    return jnp.maximum(v, 0.0) + jnp.log1p(jnp.exp(-jnp.abs(v)))


def _ada_kernel(c_ref, w_ref, b_ref, o_ref):
    a = _silu(c_ref[...])
    o_ref[...] = jnp.dot(a, w_ref[...], preferred_element_type=F32,
                         precision=lax.Precision.HIGHEST) + b_ref[...]


def _ada_call(cvec, w_ada, b_ada):
    depth, d, nm = w_ada.shape
    tn = 1024 if nm % 1024 == 0 else nm
    return pl.pallas_call(
        _ada_kernel,
        grid=(depth, nm // tn),
        in_specs=[
            pl.BlockSpec((SUBLANES, d), lambda l, j: (0, 0)),
            pl.BlockSpec((None, d, tn), lambda l, j: (l, 0, j)),
            pl.BlockSpec((None, 1, tn), lambda l, j: (l, 0, j)),
        ],
        out_specs=pl.BlockSpec((None, SUBLANES, tn), lambda l, j: (l, 0, j)),
        out_shape=jax.ShapeDtypeStruct((depth, SUBLANES, nm), F32),
        compiler_params=_cparams("arbitrary", "arbitrary"),
        name="adaln",
    )(cvec, w_ada, b_ada.reshape(depth, 1, nm))


def _mm_kernel(*refs, n_epi, epi_fn, tm):
    x_ref, w_ref = refs[0], refs[1]
    epi = refs[2:2 + n_epi]
    o_ref = refs[2 + n_epi]
    wb_ref = refs[3 + n_epi]
    i = pl.program_id(1)

    @pl.when(i == 0)
    def _():
        wb_ref[...] = w_ref[...].astype(BF16)

    if x_ref.shape[0] == tm:
        x = x_ref[...]
    else:
        x = x_ref[pl.ds(pl.multiple_of(i * tm, tm), tm), :]
    acc = jnp.dot(x, wb_ref[...], preferred_element_type=F32)
    if epi_fn is not None:
        acc = epi_fn(acc, *[r[...] for r in epi])
    if len(o_ref.shape) == 3:
        for jj in range(o_ref.shape[0]):
            o_ref[jj] = acc[:, jj * LANES:(jj + 1) * LANES].astype(o_ref.dtype)
    else:
        o_ref[...] = acc.astype(o_ref.dtype)


def _mm_tiles(rows, k, n_out, n_epi, out_bytes):
    best = None
    for tm in range(LANES, rows + 1, LANES):
        if rows % tm:
            continue
        for tn in (1024, 512, 256, 128):
            if n_out % tn:
                continue
            for resident in (False, True):
                x_vmem = rows * k * 2 if resident else 2 * tm * k * 2
                vmem = (x_vmem + 2 * k * tn * 4 + k * tn * 2 + 2 * tm * tn * out_bytes
                        + n_epi * 3 * tm * tn * 4 + tm * tn * 4)
                if vmem > MM_VMEM_BUDGET:
                    continue
                x_reads = 1 if resident else n_out // tn
                t_mxu = 2.0 * rows * k * n_out / MXU_FLOPS * (1.0 + MXU_WEIGHT_ROWS / tm)
                if tn < MXU_WEIGHT_ROWS:
                    t_mxu *= 2.0
                t_hbm = (rows * k * 2.0 * x_reads + k * n_out * 4.0
                         + rows * n_out * (out_bytes + 4.0 * n_epi)) / HBM_BYTES_PER_S
                t = t_mxu + 0.5 * t_hbm + (rows // tm) * (n_out // tn) * GRID_STEP_S
                if best is None or t < best[0]:
                    best = (t, tm, tn, resident)
    assert best is not None
    return best[1:]


def _mm_call(x, w, layer, col_off, n_out, *, rows, name, out_dtype=F32,
             epi_fn=None, epi_cols=(), epi_args=(), slab_major=False):
    k = x.shape[1]
    tm, tn, resident = _mm_tiles(rows, k, n_out, len(epi_args), jnp.dtype(out_dtype).itemsize)
    assert col_off % LANES == 0
    kern = functools.partial(_mm_kernel, n_epi=len(epi_args), epi_fn=epi_fn, tm=tm)
    if resident and rows > tm:
        x_spec = pl.BlockSpec((rows, k), lambda j, i: (0, 0), pipeline_mode=pl.Buffered(1))
    else:
        x_spec = pl.BlockSpec((tm, k), lambda j, i: (i, 0))
    epi_specs = [pl.BlockSpec((tm, tn), functools.partial(
        lambda j, i, off: (i, j + off), off=c // tn)) for c in epi_cols]
    assert all(c % tn == 0 for c in epi_cols)
    if slab_major:
        out_spec = pl.BlockSpec((tn // LANES, tm, LANES), lambda j, i: (j, i, 0))
        out_shape = jax.ShapeDtypeStruct((n_out // LANES, rows, LANES), out_dtype)
    else:
        out_spec = pl.BlockSpec((tm, tn), lambda j, i: (i, j))
        out_shape = jax.ShapeDtypeStruct((rows, n_out), out_dtype)
    return pl.pallas_call(
        kern,
        grid=(n_out // tn, rows // tm),
        in_specs=[x_spec,
                  pl.BlockSpec((pl.Element(k), pl.Element(tn)),
                               lambda j, i: (layer * k, pl.multiple_of(col_off + j * tn, LANES))),
                  *epi_specs],
        out_specs=out_spec,
        out_shape=out_shape,
        scratch_shapes=[pltpu.VMEM((k, tn), BF16)],
        compiler_params=_cparams("arbitrary", "arbitrary"),
        name=name,
    )(x, w.reshape(-1, w.shape[-1]), *epi_args)


def _row_sources(xs, tm, n_lat_tiles):
    if not isinstance(xs, tuple):
        return [pl.BlockSpec((tm, xs.shape[1]), lambda i, *_: (i, 0))], [xs]
    lat, ctx = xs
    d = lat.shape[1]
    return ([pl.BlockSpec((tm, d), lambda i, *_: (jnp.minimum(i, n_lat_tiles - 1), 0)),
             pl.BlockSpec((tm, d), lambda i, *_: (jnp.maximum(i - n_lat_tiles, 0), 0))],
            [lat, ctx])


def _row_tile(src_refs, n_lat_tiles):
    if len(src_refs) == 1:
        return src_refs[0][...]
    return jnp.where(pl.program_id(0) < n_lat_tiles, src_refs[0][...], src_refs[1][...])


def _modulate_kernel(*refs, n_src, n_lat_tiles):
    mod_ref, o_ref = refs[n_src:]
    mod = mod_ref[...]
    x = _row_tile(refs[:n_src], n_lat_tiles)
    o_ref[...] = (x * (1.0 + mod[1:2, :]) + mod[0:1, :]).astype(o_ref.dtype)


def _modulate_call(xs, mods, rows, n_lat_tiles):
    tm = ROW_TILE
    src_specs, src_args = _row_sources(xs, tm, n_lat_tiles)
    d = src_args[0].shape[1]
    return pl.pallas_call(
        functools.partial(_modulate_kernel, n_src=len(src_args), n_lat_tiles=n_lat_tiles),
        grid=(rows // tm,),
        in_specs=[*src_specs,
                  pl.BlockSpec((None, N_MOD, d),
                               lambda i: (jnp.where(i >= n_lat_tiles, 1, 0), 0, 0))],
        out_specs=pl.BlockSpec((tm, d), lambda i: (i, 0)),
        out_shape=jax.ShapeDtypeStruct((rows, d), BF16),
        compiler_params=_cparams("parallel"),
        name="modulate",
    )(*src_args, mods)


def _ssd_chunk_of_step(s, direction, nl, ncc):
    if direction == 0:
        return jnp.where(s < ncc, nl + s, s - ncc)
    return jnp.where(s < ncc, nl + ncc - 1 - s, nl - 1 - (s - ncc))


def _ssd_kernel(*refs, direction, nl, ncc, d_inner):
    if direction == 0:
        (xm_ref, xp_ref, xn_ref, dt_ref, cw_ref, cb_ref, dtb_ref, arow_ref, e_ref, dsk_ref,
         y_ref, xc_ref, state_ref, ext_ref) = refs
        yprev_ref = None
    else:
        (xc_ref, dt_ref, dtb_ref, arow_ref, e_ref, yprev_ref, y_ref, state_ref, ext_ref,
         ys_ref) = refs
    q = CHUNK
    d_bc = GROUPS * STATE
    d_xbc = d_inner + 2 * d_bc
    gw = d_inner // GROUPS
    hpg = gw // HEAD_DIM
    n_heads = d_inner // HEAD_DIM
    n_grp = q // SUBLANES

    s = pl.program_id(0)
    c = _ssd_chunk_of_step(s, direction, nl, ncc)

    @pl.when(s == 0)
    def _():
        state_ref[...] = jnp.zeros_like(state_ref)

    def slabs(ref3, rows):
        return jnp.concatenate([ref3[j, rows, :] for j in range(ref3.shape[0])], axis=1)

    if direction == 0:
        pad = SSM_CONV // 2
        seg_first = jnp.logical_or(c == 0, c == nl)
        seg_last = jnp.logical_or(c == nl - 1, c == nl + ncc - 1)
        every = slice(None)
        prev = jnp.where(seg_first, 0.0, slabs(xp_ref, every))
        nxt = jnp.where(seg_last, 0.0, slabs(xn_ref, every))
        sub = lax.broadcasted_iota(jnp.int32, (SUBLANES, d_xbc), 0)
        for m in range(n_grp):
            grp = slabs(xm_ref, pl.ds(m, SUBLANES, stride=n_grp))
            ext_ref[(pad + m) * SUBLANES:(pad + m + 1) * SUBLANES, :] = grp
            if m >= n_grp - pad:
                j = m - (n_grp - pad)
                halo = prev[HALO - pad + j:HALO - pad + j + 1, :]
                ext_ref[j * SUBLANES:(j + 1) * SUBLANES, :] = jnp.where(
                    sub == 0, halo, pltpu.roll(grp, 1, axis=0))
            if m < pad:
                halo = nxt[m:m + 1, :]
                ext_ref[(pad + n_grp + m) * SUBLANES:(pad + n_grp + m + 1) * SUBLANES, :] = (
                    jnp.where(sub == SUBLANES - 1, halo, pltpu.roll(grp, SUBLANES - 1, axis=0)))
        ct = 512
        for j in range(d_xbc // ct):
            cs = slice(j * ct, (j + 1) * ct)
            acc = jnp.broadcast_to(cb_ref[:, cs], (q, ct))
            for k in range(SSM_CONV):
                acc = acc + cw_ref[k:k + 1, cs] * ext_ref[k * SUBLANES:k * SUBLANES + q, cs]
            xc_ref[:, cs] = _silu(acc)

    dt_raw = jnp.concatenate([dt_ref[pl.ds(m, SUBLANES, stride=n_grp), :] for m in range(n_grp)],
                             axis=0)
    dt = _softplus(dt_raw + dtb_ref[...])
    a = dt * arow_ref[...]
    ri = lax.broadcasted_iota(jnp.int32, (q, q), 0)
    ci = lax.broadcasted_iota(jnp.int32, (q, q), 1)
    tok_i = (ri % SUBLANES) * n_grp + ri // SUBLANES
    tok_j = (ci % SUBLANES) * n_grp + ci // SUBLANES
    causal = (tok_i >= tok_j) if direction == 0 else (tok_i <= tok_j)
    cum = jnp.dot(causal.astype(F32), a, preferred_element_type=F32,
                  precision=lax.Precision.HIGHEST)
    tot = cum[q - 1:q, :] if direction == 0 else cum[0:1, :]
    e_in = jnp.exp(cum)
    w_st = dt * jnp.exp(tot - cum)
    cum_t = cum.T
    dt_t = dt.T

    stack = jnp.concatenate([w_st, e_in], axis=0)
    hi = stack.astype(BF16)
    lo = (stack - hi.astype(F32)).astype(BF16)
    spread = jnp.dot(jnp.concatenate([hi, lo], axis=1), e_ref[...],
                     preferred_element_type=F32)
    ext_ref[0:2 * q, 0:d_inner] = spread
    dec_row = q - 1 if direction == 0 else 0

    lane = lax.broadcasted_iota(jnp.int32, (q, LANES), 1)
    first_head = lane < HEAD_DIM
    hoff = direction * n_heads

    for g in range(GROUPS):
        bg = xc_ref[:, d_inner + g * STATE:d_inner + (g + 1) * STATE]
        cg = xc_ref[:, d_inner + d_bc + g * STATE:d_inner + d_bc + (g + 1) * STATE]
        bgb = bg.astype(BF16)
        cgb = cg.astype(BF16)
        scores = lax.dot_general(cgb, bgb, (((1,), (1,)), ((), ())),
                                 preferred_element_type=F32)
        gs = slice(g * gw, (g + 1) * gw)
        st = state_ref[:, gs]
        y_g = jnp.dot(cgb, st.astype(BF16), preferred_element_type=F32) * ext_ref[q:2 * q, gs]
        xg = xc_ref[:, gs]
        xw = (xg * ext_ref[0:q, gs]).astype(BF16)
        s_new = jnp.dot(bg.T.astype(BF16), xw, preferred_element_type=F32)
        state_ref[:, gs] = st * ext_ref[q + dec_row:q + dec_row + 1, gs] + s_new
        parts = []
        for p in range(hpg // 2):
            slab = xg[:, p * LANES:(p + 1) * LANES]
            y_p = None
            for half in range(2):
                h = hoff + g * hpg + 2 * p + half
                seg = cum[:, h:h + 1] - cum_t[h:h + 1, :]
                m = scores * jnp.exp(jnp.where(causal, seg, -jnp.inf)) * dt_t[h:h + 1, :]
                keep = first_head if half == 0 else jnp.logical_not(first_head)
                xh = jnp.where(keep, slab, 0.0).astype(BF16)
                t = jnp.dot(m.astype(BF16), xh, preferred_element_type=F32)
                y_p = t if y_p is None else y_p + t
            parts.append(y_p)
        y_g = y_g + jnp.concatenate(parts, axis=1)
        if direction == 0:
            y_ref[:, gs] = y_g + dsk_ref[:, gs] * xg
        else:
            y_g = y_g + yprev_ref[:, gs]
            for jl in range(gw // LANES):
                ys_ref[...] = y_g[:, jl * LANES:(jl + 1) * LANES]
                jg = g * (gw // LANES) + jl
                for a in range(n_grp):
                    start = (a % 2) * (q // 2) + a // 2
                    y_ref[jg, a * SUBLANES:(a + 1) * SUBLANES, :] = (
                        ys_ref[pl.ds(start, SUBLANES, stride=SUBLANES), :])


def _ssd_call(xbc, dtraw, conv_w, conv_b, dt_bias, a_row, e_mats, d_row, *, nl, ncc, d_inner):
    rows = (nl + ncc) * CHUNK
    d_xbc = d_inner + 2 * GROUPS * STATE
    nh2 = dtraw.shape[1]
    q = CHUNK
    qh = q // HALO
    nchunks = nl + ncc
    full = lambda shape: pl.BlockSpec(shape, lambda s: (0,) * len(shape))
    scratch = [pltpu.VMEM((STATE, d_inner), F32), pltpu.VMEM((2 * q, d_xbc), F32)]

    def chunk_spec(width, direction):
        cmap = functools.partial(_ssd_chunk_of_step, direction=direction, nl=nl, ncc=ncc)
        return pl.BlockSpec((q, width), lambda s: (cmap(s), 0))

    cmap = functools.partial(_ssd_chunk_of_step, direction=0, nl=nl, ncc=ncc)
    cmap_b = functools.partial(_ssd_chunk_of_step, direction=1, nl=nl, ncc=ncc)
    ns_x, ns_y = d_xbc // LANES, d_inner // LANES
    main_spec = pl.BlockSpec((ns_x, q, LANES), lambda s: (0, cmap(s), 0))
    prev_spec = pl.BlockSpec((ns_x, HALO, LANES),
                             lambda s: (0, jnp.maximum(cmap(s) * qh - 1, 0), 0))
    next_spec = pl.BlockSpec((ns_x, HALO, LANES),
                             lambda s: (0, jnp.minimum((cmap(s) + 1) * qh, nchunks * qh - 1), 0))
    y_fwd, xc = pl.pallas_call(
        functools.partial(_ssd_kernel, direction=0, nl=nl, ncc=ncc, d_inner=d_inner),
        grid=(nchunks,),
        in_specs=[main_spec, prev_spec, next_spec, chunk_spec(nh2, 0),
                  full((SUBLANES, d_xbc)), full((1, d_xbc)), full((1, nh2)), full((1, nh2)),
                  full(e_mats[0].shape), full((1, d_inner))],
        out_specs=[chunk_spec(d_inner, 0), chunk_spec(d_xbc, 0)],
        out_shape=[jax.ShapeDtypeStruct((rows, d_inner), F32),
                   jax.ShapeDtypeStruct((rows, d_xbc), F32)],
        scratch_shapes=scratch,
        compiler_params=_cparams("arbitrary"),
        name="ssd_fwd",
    )(xbc, xbc, xbc, dtraw, conv_w, conv_b, dt_bias, a_row, e_mats[0], d_row)
    return pl.pallas_call(
        functools.partial(_ssd_kernel, direction=1, nl=nl, ncc=ncc, d_inner=d_inner),
        grid=(nchunks,),
        in_specs=[chunk_spec(d_xbc, 1), chunk_spec(nh2, 1), full((1, nh2)), full((1, nh2)),
                  full(e_mats[1].shape), chunk_spec(d_inner, 1)],
        out_specs=pl.BlockSpec((ns_y, q, LANES), lambda s: (0, cmap_b(s), 0)),
        out_shape=jax.ShapeDtypeStruct((ns_y, rows, LANES), F32),
        scratch_shapes=scratch + [pltpu.VMEM((q, LANES), F32)],
        compiler_params=_cparams("arbitrary"),
        name="ssd_bwd",
    )(xc, dtraw, dt_bias, a_row, e_mats[1], y_fwd)


def _gnorm_kernel(y_ref, z_ref, w_ref, o_ref, *, d_inner):
    gw = d_inner // GROUPS
    spg = gw // LANES
    for g in range(GROUPS):
        gs = slice(g * gw, (g + 1) * gw)
        y = jnp.concatenate([y_ref[g * spg + j] for j in range(spg)], axis=1)
        h = y * _silu(z_ref[:, gs])
        ms = jnp.mean(h * h, axis=-1, keepdims=True)
        o_ref[:, gs] = (h * lax.rsqrt(ms + LN_EPS) * w_ref[:, gs]).astype(o_ref.dtype)


def _gnorm_call(y, zx, norm_w, rows, d_inner):
    tm = ROW_TILE
    return pl.pallas_call(
        functools.partial(_gnorm_kernel, d_inner=d_inner),
        grid=(rows // tm,),
        in_specs=[pl.BlockSpec((d_inner // LANES, tm, LANES), lambda i: (0, i, 0)),
                  pl.BlockSpec((tm, d_inner), lambda i: (i, 0)),
                  pl.BlockSpec((1, d_inner), lambda i: (0, 0))],
        out_specs=pl.BlockSpec((tm, d_inner), lambda i: (i, 0)),
        out_shape=jax.ShapeDtypeStruct((rows, d_inner), BF16),
        compiler_params=_cparams("parallel"),
        name="gated_rmsnorm",
    )(y, zx, norm_w)


CONV_GAP = 2 * SUBLANES


def _conv_runs(ga_ref, gb_ref, w_ref, b_ref, o_ref, pad_ref, *, row0, seq, n_seq):
    half = CONV_K // 2
    gap = CONV_GAP
    rb = pad_ref.shape[0]
    pad_ref[:, 0:gap, :] = jnp.zeros((rb, gap, LANES), F32)
    pad_ref[:, gap + seq:gap + seq + gap, :] = jnp.zeros((rb, gap, LANES), F32)
    bias = b_ref[...]

    def body(it, carry):
        r0 = pl.multiple_of(row0 + it * (rb * seq), SUBLANES)
        ga = ga_ref[pl.ds(r0, rb * seq), :]
        gb = gb_ref[pl.ds(r0, rb * seq), :]
        pad_ref[:, gap:gap + seq, :] = (ga * _sigmoid(gb)).reshape(rb, seq, LANES)
        acc = jnp.broadcast_to(bias.reshape(1, 1, LANES), (rb, seq, LANES))
        for k in range(CONV_K):
            o = gap - half + k
            acc = acc + w_ref[k:k + 1, :].reshape(1, 1, LANES) * pad_ref[:, o:o + seq, :]
        o_ref[pl.ds(r0, rb * seq), :] = acc.reshape(rb * seq, LANES)
        return carry

    lax.fori_loop(0, n_seq // rb, body, 0)


def _conv_stride(ga_ref, gb_ref, w_ref, b_ref, o_ref, pad_ref, *, n, stride):
    half = CONV_K // 2
    halo = half * stride
    pad_ref[0:halo, :] = jnp.zeros((halo, LANES), F32)
    pad_ref[halo + n:halo + n + halo, :] = jnp.zeros((halo, LANES), F32)
    tb = 256
    bias = b_ref[...]

    def fill(it, carry):
        r0 = pl.multiple_of(it * tb, tb)
        pad_ref[pl.ds(halo + r0, tb), :] = ga_ref[pl.ds(r0, tb), :] * _sigmoid(gb_ref[pl.ds(r0, tb), :])
        return carry

    lax.fori_loop(0, n // tb, fill, 0)

    def body(it, carry):
        r0 = pl.multiple_of(it * tb, tb)
        acc = jnp.broadcast_to(bias, (tb, LANES))
        for k in range(CONV_K):
            acc = acc + w_ref[k:k + 1, :] * pad_ref[pl.ds(r0 + k * stride, tb), :]
        o_ref[pl.ds(r0, tb), :] = acc
        return carry

    lax.fori_loop(0, n // tb, body, 0)


def _glu_conv_kernel(ga_ref, gb_ref, w_ref, b_ref, o_ref, pad_rows, pad_cols, pad_ctx,
                     *, n, nc, n_row_tiles):
    c = pl.program_id(0)

    @pl.when(c < n_row_tiles)
    def _():
        _conv_runs(ga_ref, gb_ref, w_ref, b_ref, o_ref, pad_rows, row0=0, seq=GRID_W,
                   n_seq=n // GRID_W)

    @pl.when(c >= n_row_tiles)
    def _():
        _conv_stride(ga_ref, gb_ref, w_ref, b_ref, o_ref, pad_cols, n=n, stride=GRID_W)

    if nc:
        _conv_runs(ga_ref, gb_ref, w_ref, b_ref, o_ref, pad_ctx, row0=n, seq=nc, n_seq=1)


def _glu_conv_call(glu, conv_w, conv_b, *, n, nc, d_conv):
    rows = n + nc
    half = d_conv // 2
    assert half % LANES == 0 and n % (4 * GRID_W) == 0
    gblk = d_conv // LANES
    wpad = jnp.zeros((32, d_conv), F32).at[:CONV_K].set(conv_w)
    bias = conv_b.reshape(1, d_conv)
    kern = functools.partial(_glu_conv_kernel, n=n, nc=nc, n_row_tiles=half // LANES)
    return pl.pallas_call(
        kern,
        grid=(d_conv // LANES,),
        in_specs=[pl.BlockSpec((rows, LANES), lambda c: (0, c)),
                  pl.BlockSpec((rows, LANES), lambda c: (0, gblk + c)),
                  pl.BlockSpec((32, LANES), lambda c: (0, c)),
                  pl.BlockSpec((1, LANES), lambda c: (0, c))],
        out_specs=pl.BlockSpec((rows, LANES), lambda c: (0, c)),
        out_shape=jax.ShapeDtypeStruct((rows, d_conv), F32),
        scratch_shapes=[pltpu.VMEM((4, GRID_W + 2 * CONV_GAP, LANES), F32),
                        pltpu.VMEM((n + 2 * (CONV_K // 2) * GRID_W, LANES), F32),
                        pltpu.VMEM((1, max(nc, SUBLANES) + 2 * CONV_GAP, LANES), F32)],
        compiler_params=_cparams("parallel"),
        name="glu_conv",
    )(glu, glu, wpad, bias)


def _lnswish_kernel(*refs, n_in):
    ins = refs[:n_in]
    g_ref, b_ref, o_ref = refs[n_in:]
    x = jnp.concatenate([r[...] for r in ins], axis=1) if n_in > 1 else ins[0][...]
    mu = jnp.mean(x, axis=-1, keepdims=True)
    xc = x - mu
    var = jnp.mean(xc * xc, axis=-1, keepdims=True)
    y = xc * lax.rsqrt(var + LN_EPS) * g_ref[...] + b_ref[...]
    o_ref[...] = _silu(y).astype(o_ref.dtype)


def _lnswish_call(parts, ln_g, ln_b):
    rows = parts[0].shape[0]
    d = sum(p.shape[1] for p in parts)
    tm = ROW_TILE
    return pl.pallas_call(
        functools.partial(_lnswish_kernel, n_in=len(parts)),
        grid=(rows // tm,),
        in_specs=[*[pl.BlockSpec((tm, p.shape[1]), lambda i: (i, 0)) for p in parts],
                  pl.BlockSpec((1, d), lambda i: (0, 0)),
                  pl.BlockSpec((1, d), lambda i: (0, 0))],
        out_specs=pl.BlockSpec((tm, d), lambda i: (i, 0)),
        out_shape=jax.ShapeDtypeStruct((rows, d), BF16),
        compiler_params=_cparams("parallel"),
        name="conv_ln_swish",
    )(*parts, ln_g.reshape(1, d), ln_b.reshape(1, d))


def _resid_ln(x, t, gate, g, b, alpha):
    v = alpha * x + gate * t
    mu = jnp.mean(v, axis=-1, keepdims=True)
    vc = v - mu
    var = jnp.mean(vc * vc, axis=-1, keepdims=True)
    return vc * lax.rsqrt(var + LN_EPS) * g + b


def _resid_ln_router_kernel(*refs, alpha, n_src, n_lat_tiles):
    t_ref, mod_ref, g_ref, b_ref, wr_ref, xo_ref, aff_ref = refs[n_src:]
    mod = mod_ref[...]
    x = _row_tile(refs[:n_src], n_lat_tiles)
    xn = _resid_ln(x, t_ref[...], mod[2:3, :], g_ref[...], b_ref[...], alpha)
    xo_ref[...] = xn
    u = xn * (1.0 + mod[4:5, :]) + mod[3:4, :]
    logits = jnp.dot(u, wr_ref[...], preferred_element_type=F32, precision=lax.Precision.HIGHEST)
    mx = jnp.max(logits, axis=-1, keepdims=True)
    ex = jnp.exp(logits - mx)
    aff_ref[...] = ex / jnp.sum(ex, axis=-1, keepdims=True)


def _resid_call(xs, t, mods, ln_g, ln_b, rows, n_lat_tiles, alpha, w_router):
    tm = ROW_TILE
    src_specs, src_args = _row_sources(xs, tm, n_lat_tiles)
    d = src_args[0].shape[1]
    row = pl.BlockSpec((tm, d), lambda i: (i, 0))
    vec = pl.BlockSpec((1, d), lambda i: (0, 0))
    mod_spec = pl.BlockSpec((None, N_MOD, d), lambda i: (jnp.where(i >= n_lat_tiles, 1, 0), 0, 0))
    ne = w_router.shape[1]
    return pl.pallas_call(
        functools.partial(_resid_ln_router_kernel, alpha=alpha, n_src=len(src_args),
                          n_lat_tiles=n_lat_tiles),
        grid=(rows // tm,),
        in_specs=[*src_specs, row, mod_spec, vec, vec, pl.BlockSpec((d, ne), lambda i: (0, 0))],
        out_specs=[row, pl.BlockSpec((tm, ne), lambda i: (i, 0))],
        out_shape=[jax.ShapeDtypeStruct((rows, d), F32),
                   jax.ShapeDtypeStruct((rows, ne), F32)],
        compiler_params=_cparams("parallel"),
        name="resid_ln_router",
    )(*src_args, t, mods, ln_g.reshape(1, d), ln_b.reshape(1, d), w_router)


def _experts_kernel(idx_ref, x_hbm, mod_ref, wg_ref, wu_ref, wd_ref, g_ref, o_ref,
                    rows_ref, xe_ref, sem, *, rc, cap_lat, nf):
    e = pl.program_id(0)
    f = pl.program_id(1)
    cap = xe_ref.shape[0]
    ne = pl.num_programs(0)
    per = _gather_rows_per_step(cap, nf)
    assert nf * per - cap <= rows_ref.shape[0] - cap

    def issue_row(ee, k):
        real = k < cap
        t = idx_ref[ee, jnp.where(real, k, cap - 1)]
        pltpu.make_async_copy(x_hbm.at[pl.ds(t, 1), :], rows_ref.at[pl.ds(k, 1), :], sem).start()

    def wait_gather():
        full = rows_ref.at[pl.ds(0, nf * per), :]
        pltpu.make_async_copy(full, full, sem).wait()

    @pl.when(jnp.logical_and(e == 0, f == 0))
    def _():
        def issue(k, carry):
            issue_row(0, k)
            return carry
        lax.fori_loop(0, nf * per, issue, 0, unroll=8)

    @pl.when(f == 0)
    def _():
        wait_gather()
        mod = mod_ref[...]
        for lo, hi, m in ((0, cap_lat, 0), (cap_lat, cap, 1)):
            if hi > lo:
                sh, sc = mod[m, 3:4, :], mod[m, 4:5, :]
                xe_ref[lo:hi, :] = (rows_ref[lo:hi, :] * (1.0 + sc) + sh).astype(BF16)
        o_ref[...] = jnp.zeros_like(o_ref)

    e_next = jnp.where(e + 1 < ne, e + 1, 0)
    for r in range(per):
        issue_row(e_next, f * per + r)

    wg = wg_ref[...].astype(BF16)
    wu = wu_ref[...].astype(BF16)
    wd = wd_ref[...].astype(BF16)
    for r0 in range(0, cap, rc):
        xe = xe_ref[r0:r0 + rc, :]
        hg = jnp.dot(xe, wg, preferred_element_type=F32)
        hu = jnp.dot(xe, wu, preferred_element_type=F32)
        h = (_silu(hg) * hu).astype(BF16)
        o_ref[r0:r0 + rc, :] += jnp.dot(h, wd, preferred_element_type=F32)

    @pl.when(f == nf - 1)
    def _():
        o_ref[...] = o_ref[...] * g_ref[...]

        @pl.when(e == ne - 1)
        def _():
            wait_gather()


def _gather_rows_per_step(cap, nf):
    per = -(-cap // nf)
    while (nf * per) % SUBLANES:
        per += 1
    return per


def _row_chunk(cap, limit=384, align=16):
    best = None
    for rc in range(align, min(cap, limit) + 1, align):
        if cap % rc == 0:
            best = rc
    assert best is not None, cap
    return best


def _experts_call(idx, xs, mods, w_gate, w_up, w_down, gsel, layer, cap_lat):
    ne, cap = idx.shape
    d = xs.shape[1]
    dexp = w_gate.shape[3]
    tf = 512 if dexp % 512 == 0 else dexp
    nf = dexp // tf
    grid_spec = pltpu.PrefetchScalarGridSpec(
        num_scalar_prefetch=1,
        grid=(ne, dexp // tf),
        in_specs=[pl.BlockSpec(memory_space=pl.ANY),
                  pl.BlockSpec((2, N_MOD, d), lambda e, f, idx: (0, 0, 0)),
                  pl.BlockSpec((None, None, d, tf), lambda e, f, idx: (layer, e, 0, f)),
                  pl.BlockSpec((None, None, d, tf), lambda e, f, idx: (layer, e, 0, f)),
                  pl.BlockSpec((None, None, tf, d), lambda e, f, idx: (layer, e, f, 0)),
                  pl.BlockSpec((None, cap, 1), lambda e, f, idx: (e, 0, 0))],
        out_specs=pl.BlockSpec((None, cap, d), lambda e, f, idx: (e, 0, 0),
                               pipeline_mode=pl.Buffered(1)),
        scratch_shapes=[pltpu.VMEM((nf * _gather_rows_per_step(cap, nf), d), F32),
                        pltpu.VMEM((cap, d), BF16), pltpu.SemaphoreType.DMA(())],
    )
    return pl.pallas_call(
        functools.partial(_experts_kernel, rc=_row_chunk(cap), cap_lat=cap_lat, nf=nf),
        grid_spec=grid_spec,
        out_shape=jax.ShapeDtypeStruct((ne, cap, d), F32),
        compiler_params=_cparams("arbitrary", "arbitrary"),
        name="experts",
    )(idx, xs, mods, w_gate, w_up, w_down, gsel)


def _route(aff, n_tok):
    cap = max(1, EC_CAPACITY_FACTOR * n_tok // N_EXPERTS)
    g, idx = lax.top_k(aff.T, cap)
    return g, idx


COMBINE_CHUNK = 256


def _combine_kernel(*refs, alpha, n_chunks, with_next):
    if with_next:
        (src_ref, bnd_ref, ye_hbm, tok_ref, x_ref, mod_ref, g_ref, b_ref, modn_ref, xo_ref,
         u_ref, buf_ref, acc_ref, cnt_ref, sem) = refs
    else:
        (src_ref, bnd_ref, ye_hbm, tok_ref, x_ref, mod_ref, g_ref, b_ref, xo_ref,
         buf_ref, acc_ref, cnt_ref, sem) = refs
    b = pl.program_id(0)
    tb = x_ref.shape[0]
    cs = COMBINE_CHUNK

    def issue(kk):
        slot = kk % 2
        base = kk * cs

        def one(i, carry):
            pltpu.make_async_copy(ye_hbm.at[pl.ds(src_ref[base + i], 1), :],
                                  buf_ref.at[slot, pl.ds(i, 1), :], sem.at[slot]).start()
            return carry
        lax.fori_loop(0, cs, one, 0, unroll=8)
        cnt_ref[0] = kk + 1

    @pl.when(b == 0)
    def _():
        cnt_ref[0] = 0
        cnt_ref[1] = 0
        issue(0)

    lo, hi = bnd_ref[b], bnd_ref[b + 1]
    k0 = lo // cs
    k1 = jnp.where(hi > lo, (hi + cs - 1) // cs, k0)
    acc_ref[...] = jnp.zeros_like(acc_ref)
    row_tok = b * tb + lax.broadcasted_iota(jnp.int32, (tb, cs), 0)

    def chunk(kk, carry):
        slot = kk % 2

        @pl.when(kk >= cnt_ref[1])
        def _():
            pltpu.make_async_copy(buf_ref.at[slot], buf_ref.at[slot], sem.at[slot]).wait()
            cnt_ref[1] = kk + 1

        @pl.when(jnp.logical_and(kk + 1 < n_chunks, kk + 1 >= cnt_ref[0]))
        def _():
            issue(kk + 1)

        toks = tok_ref[:, pl.ds(pl.multiple_of(kk * cs, cs), cs)]
        onehot = jnp.where(row_tok == toks, 1.0, 0.0).astype(BF16)
        acc_ref[...] += jnp.dot(onehot, buf_ref[slot].astype(BF16), preferred_element_type=F32)
        return carry

    lax.fori_loop(k0, k1, chunk, 0)

    mod = mod_ref[...]
    xn = _resid_ln(x_ref[...], acc_ref[...], mod[5:6, :], g_ref[...], b_ref[...], alpha)
    xo_ref[...] = xn
    if with_next:
        modn = modn_ref[...]
        u_ref[...] = (xn * (1.0 + modn[1:2, :]) + modn[0:1, :]).astype(u_ref.dtype)


def _combine_call(ye, idx, xs1, mods, ln_g, ln_b, rows, n_lat_tiles, alpha, mods_next=None):
    ne, cap, d = ye.shape
    tb = ROW_TILE
    cs = COMBINE_CHUNK
    nb = rows // tb
    n_list = ne * cap
    n_pad = -(-n_list // cs) * cs
    tok_flat = idx.reshape(-1)
    order = jnp.argsort(tok_flat).astype(jnp.int32)
    tok_sorted = jnp.take(tok_flat, order)
    bounds = jnp.searchsorted(tok_sorted, jnp.arange(nb + 1, dtype=jnp.int32) * tb).astype(jnp.int32)
    tok_pad = jnp.full((1, n_pad), -1, jnp.int32).at[0, :n_list].set(tok_sorted)
    src_pad = jnp.zeros((n_pad,), jnp.int32).at[:n_list].set(order)
    row = pl.BlockSpec((tb, d), lambda i, *_: (i, 0))
    vec = pl.BlockSpec((1, d), lambda i, *_: (0, 0))
    mod_spec = pl.BlockSpec((None, N_MOD, d),
                            lambda i, *_: (jnp.where(i >= n_lat_tiles, 1, 0), 0, 0))
    with_next = mods_next is not None
    in_specs = [pl.BlockSpec(memory_space=pl.ANY), pl.BlockSpec((1, n_pad), lambda i, *_: (0, 0)),
                row, mod_spec, vec, vec]
    args = [src_pad, bounds, ye.reshape(n_list, d), tok_pad, xs1, mods,
            ln_g.reshape(1, d), ln_b.reshape(1, d)]
    out_specs, out_shape = row, jax.ShapeDtypeStruct((rows, d), F32)
    if with_next:
        in_specs.append(mod_spec)
        args.append(mods_next)
        out_specs = [row, row]
        out_shape = [out_shape, jax.ShapeDtypeStruct((rows, d), BF16)]
    grid_spec = pltpu.PrefetchScalarGridSpec(
        num_scalar_prefetch=2,
        grid=(nb,),
        in_specs=in_specs,
        out_specs=out_specs,
        scratch_shapes=[pltpu.VMEM((2, cs, d), F32), pltpu.VMEM((tb, d), F32),
                        pltpu.SMEM((2,), jnp.int32), pltpu.SemaphoreType.DMA((2,))],
    )
    return pl.pallas_call(
        functools.partial(_combine_kernel, alpha=alpha, n_chunks=n_pad // cs,
                          with_next=with_next),
        grid_spec=grid_spec,
        out_shape=out_shape,
        compiler_params=_cparams("arbitrary"),
        name="moe_combine_ln",
    )(*args)


def _spread_matrix(n_heads, direction):
    r = jnp.arange(4 * n_heads)[:, None] % (2 * n_heads)
    ch = jnp.arange(n_heads * HEAD_DIM)[None, :] // HEAD_DIM
    return (r == ch + direction * n_heads).astype(BF16)


def kernel(x, c, ctx, c_ctx, w_ada, b_ada, w_in, ssm_conv_w, ssm_conv_b, ssm_dt_bias, ssm_a_log,
           ssm_d, ssm_norm_w, w_ssm_out, conv_dw_w, conv_dw_b, conv_ln_g, conv_ln_b, w_conv_out,
           w_o, ln1_g, ln1_b, w_router, w_exp_gate, w_exp_up, w_exp_down, ln2_g, ln2_b):
    batch, n, d = x.shape
    nc = ctx.shape[1]
    depth = w_ada.shape[0]
    assert batch == 1 and n % ROW_TILE == 0 and nc % ROW_TILE == 0 and n % nc == 0
    d_inner = 2 * d
    n_heads = d_inner // HEAD_DIM
    d_bc = GROUPS * STATE
    d_xbc = d_inner + 2 * d_bc
    d_conv = d
    o_xbc = d_inner
    o_dt = o_xbc + d_xbc
    o_glu = o_dt + 2 * n_heads
    alpha = (2 * depth) ** 0.25
    nl, ncc = n // CHUNK, nc // CHUNK
    n_lat_tiles = n // ROW_TILE

    xs = (x[0], ctx[0])
    u_in = None
    cvec = jnp.zeros((SUBLANES, d), F32).at[0].set(c[0]).at[1].set(c_ctx)
    mods_all = _ada_call(cvec, w_ada, b_ada)
    e_mats = [_spread_matrix(n_heads, dr) for dr in range(2)]

    for i in range(depth):
        last = i == depth - 1
        mods = mods_all[i, :2].reshape(2, N_MOD, d)
        rows_all = n + nc
        rows = n if last else rows_all

        if u_in is None:
            u_in = _modulate_call(xs, mods, rows_all, n_lat_tiles)
        z = _mm_call(u_in, w_in, i, 0, d_inner, rows=rows, name="proj_z")
        xbc = _mm_call(u_in, w_in, i, o_xbc, d_xbc, rows=rows_all, name="proj_xbc",
                       slab_major=True)
        dtraw = _mm_call(u_in, w_in, i, o_dt, 2 * n_heads, rows=rows_all, name="proj_dt")
        glu = _mm_call(u_in, w_in, i, o_glu, 2 * d_conv + 2 * d, rows=rows, name="proj_glu_gate")

        conv_w = jnp.zeros((SUBLANES, d_xbc), F32).at[:SSM_CONV].set(ssm_conv_w[i])
        conv_b = ssm_conv_b[i].reshape(1, d_xbc)
        dt_bias = ssm_dt_bias[i].reshape(1, 2 * n_heads)
        a_row = -jnp.exp(ssm_a_log[i].astype(F32)).reshape(1, 2 * n_heads)
        d_row = jnp.repeat(ssm_d[i], HEAD_DIM).reshape(1, d_inner)
        y = _ssd_call(xbc, dtraw, conv_w, conv_b, dt_bias, a_row, e_mats, d_row,
                      nl=nl, ncc=ncc, d_inner=d_inner)
        a_act = _gnorm_call(y, z, ssm_norm_w[i].reshape(1, d_inner), rows, d_inner)

        hcv = _glu_conv_call(glu, conv_dw_w[i], conv_dw_b[i], n=n, nc=rows - n, d_conv=d_conv)
        b_act = _lnswish_call([hcv], conv_ln_g[i], conv_ln_b[i])

        t1 = _mm_call(a_act, w_ssm_out, i, 0, d, rows=rows, name="ssm_out",
                      epi_fn=lambda acc, gt: _sigmoid(gt) * acc,
                      epi_cols=[2 * d_conv], epi_args=[glu])
        t2 = _mm_call(b_act, w_conv_out, i, 0, d, rows=rows, name="conv_out", out_dtype=BF16,
                      epi_fn=lambda acc, gt, prev: prev + _sigmoid(gt) * acc,
                      epi_cols=[2 * d_conv + d, 0], epi_args=[glu, t1])
        mix = _mm_call(t2, w_o, i, 0, d, rows=rows, name="w_o")

        xs1, aff = _resid_call(xs, mix, mods, ln1_g[i], ln1_b[i], rows, n_lat_tiles, alpha,
                               w_router=w_router[i])

        g_l, idx_l = _route(aff[:n], n)
        gsel, idx = g_l, idx_l
        if not last:
            g_c, idx_c = _route(aff[n:], nc)
            gsel = jnp.concatenate([g_l, g_c], axis=1)
            idx = jnp.concatenate([idx_l, idx_c + n], axis=1)
        ye = _experts_call(idx, xs1, mods, w_exp_gate, w_exp_up, w_exp_down, gsel[..., None], i,
                           idx_l.shape[1])
        if last:
            xs = _combine_call(ye, idx, xs1, mods, ln2_g[i], ln2_b[i], rows, n_lat_tiles, alpha)
        else:
            mods_next = mods_all[i + 1, :2].reshape(2, N_MOD, d)
            xs, u_in = _combine_call(ye, idx, xs1, mods, ln2_g[i], ln2_b[i], rows, n_lat_tiles,
                                     alpha, mods_next=mods_next)

    return xs[:n].reshape(1, n, d)
```

```python
import functools

import jax
import jax.numpy as jnp
from jax import lax
from jax.experimental import pallas as pl
from jax.experimental.pallas import tpu as pltpu

F32 = jnp.float32
BF16 = jnp.bfloat16

HEAD_DIM = 64
GROUPS = 8
STATE = 128
SSM_CONV = 5
CHUNK = 128
CONV_K = 31
GRID_W = 64
N_EXPERTS = 16
EC_CAPACITY_FACTOR = 2
N_MOD = 6
LN_EPS = 1e-5

LANES = 128
SUBLANES = 8
VMEM_LIMIT = 60 * 1024 * 1024
MM_VMEM_BUDGET = 47 * 1024 * 1024
MXU_FLOPS = 1.1e15
MXU_WEIGHT_ROWS = 256
HBM_BYTES_PER_S = 3.3e12
GRID_STEP_S = 0.35e-6

ROW_TILE = 256
HALO = SUBLANES


def _cparams(*sem):
    return pltpu.CompilerParams(dimension_semantics=sem, vmem_limit_bytes=VMEM_LIMIT)


def _sigmoid(v):
    return jax.nn.sigmoid(v)


def _silu(v):
    return v * jax.nn.sigmoid(v)


def _softplus(v):
    return jnp.maximum(v, 0.0) + jnp.log1p(jnp.exp(-jnp.abs(v)))


def _ada_kernel(c_ref, w_ref, b_ref, o_ref):
    a = _silu(c_ref[...])
    o_ref[...] = jnp.dot(a, w_ref[...], preferred_element_type=F32,
                         precision=lax.Precision.HIGHEST) + b_ref[...]


def _ada_call(cvec, w_ada, b_ada):
    depth, d, nm = w_ada.shape
    tn = 1024 if nm % 1024 == 0 else nm
    return pl.pallas_call(
        _ada_kernel,
        grid=(depth, nm // tn),
        in_specs=[
            pl.BlockSpec((SUBLANES, d), lambda l, j: (0, 0)),
            pl.BlockSpec((None, d, tn), lambda l, j: (l, 0, j)),
            pl.BlockSpec((None, 1, tn), lambda l, j: (l, 0, j)),
        ],
        out_specs=pl.BlockSpec((None, SUBLANES, tn), lambda l, j: (l, 0, j)),
        out_shape=jax.ShapeDtypeStruct((depth, SUBLANES, nm), F32),
        compiler_params=_cparams("arbitrary", "arbitrary"),
        name="adaln",
    )(cvec, w_ada, b_ada.reshape(depth, 1, nm))


def _mm_kernel(*refs, n_epi, epi_fn, tm):
    x_ref, w_ref = refs[0], refs[1]
    epi = refs[2:2 + n_epi]
    o_ref = refs[2 + n_epi]
    wb_ref = refs[3 + n_epi]
    i = pl.program_id(1)

    @pl.when(i == 0)
    def _():
        wb_ref[...] = w_ref[...].astype(BF16)

    if x_ref.shape[0] == tm:
        x = x_ref[...]
    else:
        x = x_ref[pl.ds(pl.multiple_of(i * tm, tm), tm), :]
    acc = jnp.dot(x, wb_ref[...], preferred_element_type=F32)
    if epi_fn is not None:
        acc = epi_fn(acc, *[r[...] for r in epi])
    if len(o_ref.shape) == 3:
        for jj in range(o_ref.shape[0]):
            o_ref[jj] = acc[:, jj * LANES:(jj + 1) * LANES].astype(o_ref.dtype)
    else:
        o_ref[...] = acc.astype(o_ref.dtype)


def _mm_tiles(rows, k, n_out, n_epi, out_bytes):
    best = None
    for tm in range(LANES, rows + 1, LANES):
        if rows % tm:
            continue
        for tn in (1024, 512, 256, 128):
            if n_out % tn:
                continue
            for resident in (False, True):
                x_vmem = rows * k * 2 if resident else 2 * tm * k * 2
                vmem = (x_vmem + 2 * k * tn * 4 + k * tn * 2 + 2 * tm * tn * out_bytes
                        + n_epi * 3 * tm * tn * 4 + tm * tn * 4)
                if vmem > MM_VMEM_BUDGET:
                    continue
                x_reads = 1 if resident else n_out // tn
                t_mxu = 2.0 * rows * k * n_out / MXU_FLOPS * (1.0 + MXU_WEIGHT_ROWS / tm)
                if tn < MXU_WEIGHT_ROWS:
                    t_mxu *= 2.0
                t_hbm = (rows * k * 2.0 * x_reads + k * n_out * 4.0
                         + rows * n_out * (out_bytes + 4.0 * n_epi)) / HBM_BYTES_PER_S
                t = t_mxu + 0.5 * t_hbm + (rows // tm) * (n_out // tn) * GRID_STEP_S
                if best is None or t < best[0]:
                    best = (t, tm, tn, resident)
    assert best is not None
    return best[1:]


def _mm_call(x, w, layer, col_off, n_out, *, rows, name, out_dtype=F32,
             epi_fn=None, epi_cols=(), epi_args=(), slab_major=False):
    k = x.shape[1]
    tm, tn, resident = _mm_tiles(rows, k, n_out, len(epi_args), jnp.dtype(out_dtype).itemsize)
    assert col_off % LANES == 0
    kern = functools.partial(_mm_kernel, n_epi=len(epi_args), epi_fn=epi_fn, tm=tm)
    if resident and rows > tm:
        x_spec = pl.BlockSpec((rows, k), lambda j, i: (0, 0), pipeline_mode=pl.Buffered(1))
    else:
        x_spec = pl.BlockSpec((tm, k), lambda j, i: (i, 0))
    epi_specs = [pl.BlockSpec((tm, tn), functools.partial(
        lambda j, i, off: (i, j + off), off=c // tn)) for c in epi_cols]
    assert all(c % tn == 0 for c in epi_cols)
    if slab_major:
        out_spec = pl.BlockSpec((tn // LANES, tm, LANES), lambda j, i: (j, i, 0))
        out_shape = jax.ShapeDtypeStruct((n_out // LANES, rows, LANES), out_dtype)
    else:
        out_spec = pl.BlockSpec((tm, tn), lambda j, i: (i, j))
        out_shape = jax.ShapeDtypeStruct((rows, n_out), out_dtype)
    return pl.pallas_call(
        kern,
        grid=(n_out // tn, rows // tm),
        in_specs=[x_spec,
                  pl.BlockSpec((pl.Element(k), pl.Element(tn)),
                               lambda j, i: (layer * k, pl.multiple_of(col_off + j * tn, LANES))),
                  *epi_specs],
        out_specs=out_spec,
        out_shape=out_shape,
        scratch_shapes=[pltpu.VMEM((k, tn), BF16)],
        compiler_params=_cparams("arbitrary", "arbitrary"),
        name=name,
    )(x, w.reshape(-1, w.shape[-1]), *epi_args)


def _row_sources(xs, tm, n_lat_tiles):
    if not isinstance(xs, tuple):
        return [pl.BlockSpec((tm, xs.shape[1]), lambda i, *_: (i, 0))], [xs]
    lat, ctx = xs
    d = lat.shape[1]
    return ([pl.BlockSpec((tm, d), lambda i, *_: (jnp.minimum(i, n_lat_tiles - 1), 0)),
             pl.BlockSpec((tm, d), lambda i, *_: (jnp.maximum(i - n_lat_tiles, 0), 0))],
            [lat, ctx])


def _row_tile(src_refs, n_lat_tiles):
    if len(src_refs) == 1:
        return src_refs[0][...]
    return jnp.where(pl.program_id(0) < n_lat_tiles, src_refs[0][...], src_refs[1][...])


def _modulate_kernel(*refs, n_src, n_lat_tiles):
    mod_ref, o_ref = refs[n_src:]
    mod = mod_ref[...]
    x = _row_tile(refs[:n_src], n_lat_tiles)
    o_ref[...] = (x * (1.0 + mod[1:2, :]) + mod[0:1, :]).astype(o_ref.dtype)


def _modulate_call(xs, mods, rows, n_lat_tiles):
    tm = ROW_TILE
    src_specs, src_args = _row_sources(xs, tm, n_lat_tiles)
    d = src_args[0].shape[1]
    return pl.pallas_call(
        functools.partial(_modulate_kernel, n_src=len(src_args), n_lat_tiles=n_lat_tiles),
        grid=(rows // tm,),
        in_specs=[*src_specs,
                  pl.BlockSpec((None, N_MOD, d),
                               lambda i: (jnp.where(i >= n_lat_tiles, 1, 0), 0, 0))],
        out_specs=pl.BlockSpec((tm, d), lambda i: (i, 0)),
        out_shape=jax.ShapeDtypeStruct((rows, d), BF16),
        compiler_params=_cparams("parallel"),
        name="modulate",
    )(*src_args, mods)


def _ssd_chunk_of_step(s, direction, nl, ncc):
    if direction == 0:
        return jnp.where(s < ncc, nl + s, s - ncc)
    return jnp.where(s < ncc, nl + ncc - 1 - s, nl - 1 - (s - ncc))


def _ssd_kernel(*refs, direction, nl, ncc, d_inner):
    if direction == 0:
        (xm_ref, xp_ref, xn_ref, dt_ref, cw_ref, cb_ref, dtb_ref, arow_ref, e_ref, dsk_ref,
         y_ref, xc_ref, state_ref, ext_ref) = refs
        yprev_ref = None
    else:
        (xc_ref, dt_ref, dtb_ref, arow_ref, e_ref, yprev_ref, y_ref, state_ref, ext_ref) = refs
    q = CHUNK
    d_bc = GROUPS * STATE
    d_xbc = d_inner + 2 * d_bc
    gw = d_inner // GROUPS
    hpg = gw // HEAD_DIM
    n_heads = d_inner // HEAD_DIM
    n_grp = q // SUBLANES

    s = pl.program_id(0)
    c = _ssd_chunk_of_step(s, direction, nl, ncc)

    @pl.when(s == 0)
    def _():
        state_ref[...] = jnp.zeros_like(state_ref)

    def slabs(ref3, rows):
        return jnp.concatenate([ref3[j, rows, :] for j in range(ref3.shape[0])], axis=1)

    if direction == 0:
        pad = SSM_CONV // 2
        seg_first = jnp.logical_or(c == 0, c == nl)
        seg_last = jnp.logical_or(c == nl - 1, c == nl + ncc - 1)
        every = slice(None)
        prev = jnp.where(seg_first, 0.0, slabs(xp_ref, every))
        nxt = jnp.where(seg_last, 0.0, slabs(xn_ref, every))
        sub = lax.broadcasted_iota(jnp.int32, (SUBLANES, d_xbc), 0)
        for m in range(n_grp):
            grp = slabs(xm_ref, pl.ds(m, SUBLANES, stride=n_grp))
            ext_ref[(pad + m) * SUBLANES:(pad + m + 1) * SUBLANES, :] = grp
            if m >= n_grp - pad:
                j = m - (n_grp - pad)
                halo = prev[HALO - pad + j:HALO - pad + j + 1, :]
                ext_ref[j * SUBLANES:(j + 1) * SUBLANES, :] = jnp.where(
                    sub == 0, halo, pltpu.roll(grp, 1, axis=0))
            if m < pad:
                halo = nxt[m:m + 1, :]
                ext_ref[(pad + n_grp + m) * SUBLANES:(pad + n_grp + m + 1) * SUBLANES, :] = (
                    jnp.where(sub == SUBLANES - 1, halo, pltpu.roll(grp, SUBLANES - 1, axis=0)))
        ct = 512
        for j in range(d_xbc // ct):
            cs = slice(j * ct, (j + 1) * ct)
            acc = jnp.broadcast_to(cb_ref[:, cs], (q, ct))
            for k in range(SSM_CONV):
                acc = acc + cw_ref[k:k + 1, cs] * ext_ref[k * SUBLANES:k * SUBLANES + q, cs]
            xc_ref[:, cs] = _silu(acc)

    dt_raw = jnp.concatenate([dt_ref[pl.ds(m, SUBLANES, stride=n_grp), :] for m in range(n_grp)],
                             axis=0)
    dt = _softplus(dt_raw + dtb_ref[...])
    a = dt * arow_ref[...]
    ri = lax.broadcasted_iota(jnp.int32, (q, q), 0)
    ci = lax.broadcasted_iota(jnp.int32, (q, q), 1)
    tok_i = (ri % SUBLANES) * n_grp + ri // SUBLANES
    tok_j = (ci % SUBLANES) * n_grp + ci // SUBLANES
    causal = (tok_i >= tok_j) if direction == 0 else (tok_i <= tok_j)
    cum = jnp.dot(causal.astype(F32), a, preferred_element_type=F32,
                  precision=lax.Precision.HIGHEST)
    tot = cum[q - 1:q, :] if direction == 0 else cum[0:1, :]
    e_in = jnp.exp(cum)
    w_st = dt * jnp.exp(tot - cum)
    cum_t = cum.T
    dt_t = dt.T

    stack = jnp.concatenate([w_st, e_in], axis=0)
    hi = stack.astype(BF16)
    lo = (stack - hi.astype(F32)).astype(BF16)
    spread = jnp.dot(jnp.concatenate([hi, lo], axis=1), e_ref[...],
                     preferred_element_type=F32)
    ext_ref[0:2 * q, 0:d_inner] = spread
    dec_row = q - 1 if direction == 0 else 0

    lane = lax.broadcasted_iota(jnp.int32, (q, LANES), 1)
    first_head = lane < HEAD_DIM
    hoff = direction * n_heads

    for g in range(GROUPS):
        bg = xc_ref[:, d_inner + g * STATE:d_inner + (g + 1) * STATE]
        cg = xc_ref[:, d_inner + d_bc + g * STATE:d_inner + d_bc + (g + 1) * STATE]
        bgb = bg.astype(BF16)
        cgb = cg.astype(BF16)
        scores = lax.dot_general(cgb, bgb, (((1,), (1,)), ((), ())),
                                 preferred_element_type=F32)
        gs = slice(g * gw, (g + 1) * gw)
        st = state_ref[:, gs]
        y_g = jnp.dot(cgb, st.astype(BF16), preferred_element_type=F32) * ext_ref[q:2 * q, gs]
        xg = xc_ref[:, gs]
        xw = (xg * ext_ref[0:q, gs]).astype(BF16)
        s_new = jnp.dot(bg.T.astype(BF16), xw, preferred_element_type=F32)
        state_ref[:, gs] = st * ext_ref[q + dec_row:q + dec_row + 1, gs] + s_new
        parts = []
        for p in range(hpg // 2):
            slab = xg[:, p * LANES:(p + 1) * LANES]
            y_p = None
            for half in range(2):
                h = hoff + g * hpg + 2 * p + half
                seg = cum[:, h:h + 1] - cum_t[h:h + 1, :]
                m = scores * jnp.exp(jnp.where(causal, seg, -jnp.inf)) * dt_t[h:h + 1, :]
                keep = first_head if half == 0 else jnp.logical_not(first_head)
                xh = jnp.where(keep, slab, 0.0).astype(BF16)
                t = jnp.dot(m.astype(BF16), xh, preferred_element_type=F32)
                y_p = t if y_p is None else y_p + t
            parts.append(y_p)
        y_g = y_g + jnp.concatenate(parts, axis=1)
        if direction == 0:
            y_ref[:, gs] = y_g + dsk_ref[:, gs] * xg
        else:
            y_g = y_g + yprev_ref[:, gs]
            for jl in range(gw // LANES):
                y_ref[g * (gw // LANES) + jl] = y_g[:, jl * LANES:(jl + 1) * LANES]


def _ssd_call(xbc, dtraw, conv_w, conv_b, dt_bias, a_row, e_mats, d_row, *, nl, ncc, d_inner):
    rows = (nl + ncc) * CHUNK
    d_xbc = d_inner + 2 * GROUPS * STATE
    nh2 = dtraw.shape[1]
    q = CHUNK
    qh = q // HALO
    nchunks = nl + ncc
    full = lambda shape: pl.BlockSpec(shape, lambda s: (0,) * len(shape))
    scratch = [pltpu.VMEM((STATE, d_inner), F32), pltpu.VMEM((2 * q, d_xbc), F32)]

    def chunk_spec(width, direction):
        cmap = functools.partial(_ssd_chunk_of_step, direction=direction, nl=nl, ncc=ncc)
        return pl.BlockSpec((q, width), lambda s: (cmap(s), 0))

    cmap = functools.partial(_ssd_chunk_of_step, direction=0, nl=nl, ncc=ncc)
    cmap_b = functools.partial(_ssd_chunk_of_step, direction=1, nl=nl, ncc=ncc)
    ns_x, ns_y = d_xbc // LANES, d_inner // LANES
    main_spec = pl.BlockSpec((ns_x, q, LANES), lambda s: (0, cmap(s), 0))
    prev_spec = pl.BlockSpec((ns_x, HALO, LANES),
                             lambda s: (0, jnp.maximum(cmap(s) * qh - 1, 0), 0))
    next_spec = pl.BlockSpec((ns_x, HALO, LANES),
                             lambda s: (0, jnp.minimum((cmap(s) + 1) * qh, nchunks * qh - 1), 0))
    y_fwd, xc = pl.pallas_call(
        functools.partial(_ssd_kernel, direction=0, nl=nl, ncc=ncc, d_inner=d_inner),
        grid=(nchunks,),
        in_specs=[main_spec, prev_spec, next_spec, chunk_spec(nh2, 0),
                  full((SUBLANES, d_xbc)), full((1, d_xbc)), full((1, nh2)), full((1, nh2)),
                  full(e_mats[0].shape), full((1, d_inner))],
        out_specs=[chunk_spec(d_inner, 0), chunk_spec(d_xbc, 0)],
        out_shape=[jax.ShapeDtypeStruct((rows, d_inner), F32),
                   jax.ShapeDtypeStruct((rows, d_xbc), F32)],
        scratch_shapes=scratch,
        compiler_params=_cparams("arbitrary"),
        name="ssd_fwd",
    )(xbc, xbc, xbc, dtraw, conv_w, conv_b, dt_bias, a_row, e_mats[0], d_row)
    return pl.pallas_call(
        functools.partial(_ssd_kernel, direction=1, nl=nl, ncc=ncc, d_inner=d_inner),
        grid=(nchunks,),
        in_specs=[chunk_spec(d_xbc, 1), chunk_spec(nh2, 1), full((1, nh2)), full((1, nh2)),
                  full(e_mats[1].shape), chunk_spec(d_inner, 1)],
        out_specs=pl.BlockSpec((ns_y, q, LANES), lambda s: (0, cmap_b(s), 0)),
        out_shape=jax.ShapeDtypeStruct((ns_y, rows, LANES), F32),
        scratch_shapes=scratch,
        compiler_params=_cparams("arbitrary"),
        name="ssd_bwd",
    )(xc, dtraw, dt_bias, a_row, e_mats[1], y_fwd)


def _gnorm_kernel(y_ref, z_ref, w_ref, o_ref, *, d_inner):
    gw = d_inner // GROUPS
    spg = gw // LANES
    tm = z_ref.shape[0]
    n_grp = CHUNK // SUBLANES

    def token_order(slab):
        parts = []
        for c0 in range(0, tm, CHUNK):
            for a in range(n_grp):
                start = c0 + (a % 2) * (CHUNK // 2) + a // 2
                parts.append(y_ref[slab, pl.ds(start, SUBLANES, stride=SUBLANES), :])
        return jnp.concatenate(parts, axis=0)

    for g in range(GROUPS):
        gs = slice(g * gw, (g + 1) * gw)
        y = jnp.concatenate([token_order(g * spg + j) for j in range(spg)], axis=1)
        h = y * _silu(z_ref[:, gs])
        ms = jnp.mean(h * h, axis=-1, keepdims=True)
        o_ref[:, gs] = (h * lax.rsqrt(ms + LN_EPS) * w_ref[:, gs]).astype(o_ref.dtype)


def _gnorm_call(y, zx, norm_w, rows, d_inner):
    tm = ROW_TILE
    return pl.pallas_call(
        functools.partial(_gnorm_kernel, d_inner=d_inner),
        grid=(rows // tm,),
        in_specs=[pl.BlockSpec((d_inner // LANES, tm, LANES), lambda i: (0, i, 0)),
                  pl.BlockSpec((tm, d_inner), lambda i: (i, 0)),
                  pl.BlockSpec((1, d_inner), lambda i: (0, 0))],
        out_specs=pl.BlockSpec((tm, d_inner), lambda i: (i, 0)),
        out_shape=jax.ShapeDtypeStruct((rows, d_inner), BF16),
        compiler_params=_cparams("parallel"),
        name="gated_rmsnorm",
    )(y, zx, norm_w)


CONV_GAP = 2 * SUBLANES


def _conv_runs(ga_ref, gb_ref, w_ref, b_ref, o_ref, pad_ref, *, row0, seq, n_seq):
    half = CONV_K // 2
    gap = CONV_GAP
    rb = pad_ref.shape[0]
    pad_ref[:, 0:gap, :] = jnp.zeros((rb, gap, LANES), F32)
    pad_ref[:, gap + seq:gap + seq + gap, :] = jnp.zeros((rb, gap, LANES), F32)
    bias = b_ref[...]

    def body(it, carry):
        r0 = pl.multiple_of(row0 + it * (rb * seq), SUBLANES)
        ga = ga_ref[pl.ds(r0, rb * seq), :]
        gb = gb_ref[pl.ds(r0, rb * seq), :]
        pad_ref[:, gap:gap + seq, :] = (ga * _sigmoid(gb)).reshape(rb, seq, LANES)
        acc = jnp.broadcast_to(bias.reshape(1, 1, LANES), (rb, seq, LANES))
        for k in range(CONV_K):
            o = gap - half + k
            acc = acc + w_ref[k:k + 1, :].reshape(1, 1, LANES) * pad_ref[:, o:o + seq, :]
        o_ref[pl.ds(r0, rb * seq), :] = acc.reshape(rb * seq, LANES)
        return carry

    lax.fori_loop(0, n_seq // rb, body, 0)


def _conv_stride(ga_ref, gb_ref, w_ref, b_ref, o_ref, pad_ref, *, n, stride):
    half = CONV_K // 2
    halo = half * stride
    pad_ref[0:halo, :] = jnp.zeros((halo, LANES), F32)
    pad_ref[halo + n:halo + n + halo, :] = jnp.zeros((halo, LANES), F32)
    tb = 256
    bias = b_ref[...]

    def fill(it, carry):
        r0 = pl.multiple_of(it * tb, tb)
        pad_ref[pl.ds(halo + r0, tb), :] = ga_ref[pl.ds(r0, tb), :] * _sigmoid(gb_ref[pl.ds(r0, tb), :])
        return carry

    lax.fori_loop(0, n // tb, fill, 0)

    def body(it, carry):
        r0 = pl.multiple_of(it * tb, tb)
        acc = jnp.broadcast_to(bias, (tb, LANES))
        for k in range(CONV_K):
            acc = acc + w_ref[k:k + 1, :] * pad_ref[pl.ds(r0 + k * stride, tb), :]
        o_ref[pl.ds(r0, tb), :] = acc
        return carry

    lax.fori_loop(0, n // tb, body, 0)


def _glu_conv_kernel(ga_ref, gb_ref, w_ref, b_ref, o_ref, pad_rows, pad_cols, pad_ctx,
                     *, n, nc, n_row_tiles):
    c = pl.program_id(0)

    @pl.when(c < n_row_tiles)
    def _():
        _conv_runs(ga_ref, gb_ref, w_ref, b_ref, o_ref, pad_rows, row0=0, seq=GRID_W,
                   n_seq=n // GRID_W)

    @pl.when(c >= n_row_tiles)
    def _():
        _conv_stride(ga_ref, gb_ref, w_ref, b_ref, o_ref, pad_cols, n=n, stride=GRID_W)

    if nc:
        _conv_runs(ga_ref, gb_ref, w_ref, b_ref, o_ref, pad_ctx, row0=n, seq=nc, n_seq=1)


def _glu_conv_call(glu, conv_w, conv_b, *, n, nc, d_conv):
    rows = n + nc
    half = d_conv // 2
    assert half % LANES == 0 and n % (4 * GRID_W) == 0
    gblk = d_conv // LANES
    wpad = jnp.zeros((32, d_conv), F32).at[:CONV_K].set(conv_w)
    bias = conv_b.reshape(1, d_conv)
    kern = functools.partial(_glu_conv_kernel, n=n, nc=nc, n_row_tiles=half // LANES)
    return pl.pallas_call(
        kern,
        grid=(d_conv // LANES,),
        in_specs=[pl.BlockSpec((rows, LANES), lambda c: (0, c)),
                  pl.BlockSpec((rows, LANES), lambda c: (0, gblk + c)),
                  pl.BlockSpec((32, LANES), lambda c: (0, c)),
                  pl.BlockSpec((1, LANES), lambda c: (0, c))],
        out_specs=pl.BlockSpec((rows, LANES), lambda c: (0, c)),
        out_shape=jax.ShapeDtypeStruct((rows, d_conv), F32),
        scratch_shapes=[pltpu.VMEM((4, GRID_W + 2 * CONV_GAP, LANES), F32),
                        pltpu.VMEM((n + 2 * (CONV_K // 2) * GRID_W, LANES), F32),
                        pltpu.VMEM((1, max(nc, SUBLANES) + 2 * CONV_GAP, LANES), F32)],
        compiler_params=_cparams("parallel"),
        name="glu_conv",
    )(glu, glu, wpad, bias)


def _lnswish_kernel(*refs, n_in):
    ins = refs[:n_in]
    g_ref, b_ref, o_ref = refs[n_in:]
    x = jnp.concatenate([r[...] for r in ins], axis=1) if n_in > 1 else ins[0][...]
    mu = jnp.mean(x, axis=-1, keepdims=True)
    xc = x - mu
    var = jnp.mean(xc * xc, axis=-1, keepdims=True)
    y = xc * lax.rsqrt(var + LN_EPS) * g_ref[...] + b_ref[...]
    o_ref[...] = _silu(y).astype(o_ref.dtype)


def _lnswish_call(parts, ln_g, ln_b):
    rows = parts[0].shape[0]
    d = sum(p.shape[1] for p in parts)
    tm = ROW_TILE
    return pl.pallas_call(
        functools.partial(_lnswish_kernel, n_in=len(parts)),
        grid=(rows // tm,),
        in_specs=[*[pl.BlockSpec((tm, p.shape[1]), lambda i: (i, 0)) for p in parts],
                  pl.BlockSpec((1, d), lambda i: (0, 0)),
                  pl.BlockSpec((1, d), lambda i: (0, 0))],
        out_specs=pl.BlockSpec((tm, d), lambda i: (i, 0)),
        out_shape=jax.ShapeDtypeStruct((rows, d), BF16),
        compiler_params=_cparams("parallel"),
        name="conv_ln_swish",
    )(*parts, ln_g.reshape(1, d), ln_b.reshape(1, d))


def _resid_ln(x, t, gate, g, b, alpha):
    v = alpha * x + gate * t
    mu = jnp.mean(v, axis=-1, keepdims=True)
    vc = v - mu
    var = jnp.mean(vc * vc, axis=-1, keepdims=True)
    return vc * lax.rsqrt(var + LN_EPS) * g + b


def _resid_ln_router_kernel(*refs, alpha, n_src, n_lat_tiles):
    t_ref, mod_ref, g_ref, b_ref, wr_ref, xo_ref, aff_ref = refs[n_src:]
    mod = mod_ref[...]
    x = _row_tile(refs[:n_src], n_lat_tiles)
    xn = _resid_ln(x, t_ref[...], mod[2:3, :], g_ref[...], b_ref[...], alpha)
    xo_ref[...] = xn
    u = xn * (1.0 + mod[4:5, :]) + mod[3:4, :]
    logits = jnp.dot(u, wr_ref[...], preferred_element_type=F32, precision=lax.Precision.HIGHEST)
    mx = jnp.max(logits, axis=-1, keepdims=True)
    ex = jnp.exp(logits - mx)
    aff_ref[...] = ex / jnp.sum(ex, axis=-1, keepdims=True)


def _resid_call(xs, t, mods, ln_g, ln_b, rows, n_lat_tiles, alpha, w_router):
    tm = ROW_TILE
    src_specs, src_args = _row_sources(xs, tm, n_lat_tiles)
    d = src_args[0].shape[1]
    row = pl.BlockSpec((tm, d), lambda i: (i, 0))
    vec = pl.BlockSpec((1, d), lambda i: (0, 0))
    mod_spec = pl.BlockSpec((None, N_MOD, d), lambda i: (jnp.where(i >= n_lat_tiles, 1, 0), 0, 0))
    ne = w_router.shape[1]
    return pl.pallas_call(
        functools.partial(_resid_ln_router_kernel, alpha=alpha, n_src=len(src_args),
                          n_lat_tiles=n_lat_tiles),
        grid=(rows // tm,),
        in_specs=[*src_specs, row, mod_spec, vec, vec, pl.BlockSpec((d, ne), lambda i: (0, 0))],
        out_specs=[row, pl.BlockSpec((tm, ne), lambda i: (i, 0))],
        out_shape=[jax.ShapeDtypeStruct((rows, d), F32),
                   jax.ShapeDtypeStruct((rows, ne), F32)],
        compiler_params=_cparams("parallel"),
        name="resid_ln_router",
    )(*src_args, t, mods, ln_g.reshape(1, d), ln_b.reshape(1, d), w_router)


def _experts_kernel(idx_ref, x_hbm, mod_ref, wg_ref, wu_ref, wd_ref, g_ref, o_ref,
                    rows_ref, xe_ref, sem, *, rc, cap_lat, nf):
    e = pl.program_id(0)
    f = pl.program_id(1)
    cap = xe_ref.shape[0]
    ne = pl.num_programs(0)
    per = _gather_rows_per_step(cap, nf)
    assert nf * per - cap <= rows_ref.shape[0] - cap

    def issue_row(ee, k):
        real = k < cap
        t = idx_ref[ee, jnp.where(real, k, cap - 1)]
        pltpu.make_async_copy(x_hbm.at[pl.ds(t, 1), :], rows_ref.at[pl.ds(k, 1), :], sem).start()

    def wait_gather():
        full = rows_ref.at[pl.ds(0, nf * per), :]
        pltpu.make_async_copy(full, full, sem).wait()

    @pl.when(jnp.logical_and(e == 0, f == 0))
    def _():
        def issue(k, carry):
            issue_row(0, k)
            return carry
        lax.fori_loop(0, nf * per, issue, 0, unroll=8)

    @pl.when(f == 0)
    def _():
        wait_gather()
        mod = mod_ref[...]
        for lo, hi, m in ((0, cap_lat, 0), (cap_lat, cap, 1)):
            if hi > lo:
                sh, sc = mod[m, 3:4, :], mod[m, 4:5, :]
                xe_ref[lo:hi, :] = (rows_ref[lo:hi, :] * (1.0 + sc) + sh).astype(BF16)
        o_ref[...] = jnp.zeros_like(o_ref)

    e_next = jnp.where(e + 1 < ne, e + 1, 0)
    for r in range(per):
        issue_row(e_next, f * per + r)

    wg = wg_ref[...].astype(BF16)
    wu = wu_ref[...].astype(BF16)
    wd = wd_ref[...].astype(BF16)
    for r0 in range(0, cap, rc):
        xe = xe_ref[r0:r0 + rc, :]
        hg = jnp.dot(xe, wg, preferred_element_type=F32)
        hu = jnp.dot(xe, wu, preferred_element_type=F32)
        h = (_silu(hg) * hu).astype(BF16)
        o_ref[r0:r0 + rc, :] += jnp.dot(h, wd, preferred_element_type=F32)

    @pl.when(f == nf - 1)
    def _():
        o_ref[...] = o_ref[...] * g_ref[...]

        @pl.when(e == ne - 1)
        def _():
            wait_gather()


def _gather_rows_per_step(cap, nf):
    per = -(-cap // nf)
    while (nf * per) % SUBLANES:
        per += 1
    return per


def _row_chunk(cap, limit=384, align=16):
    best = None
    for rc in range(align, min(cap, limit) + 1, align):
        if cap % rc == 0:
            best = rc
    assert best is not None, cap
    return best


def _experts_call(idx, xs, mods, w_gate, w_up, w_down, gsel, layer, cap_lat):
    ne, cap = idx.shape
    d = xs.shape[1]
    dexp = w_gate.shape[3]
    tf = 512 if dexp % 512 == 0 else dexp
    nf = dexp // tf
    grid_spec = pltpu.PrefetchScalarGridSpec(
        num_scalar_prefetch=1,
        grid=(ne, dexp // tf),
        in_specs=[pl.BlockSpec(memory_space=pl.ANY),
                  pl.BlockSpec((2, N_MOD, d), lambda e, f, idx: (0, 0, 0)),
                  pl.BlockSpec((None, None, d, tf), lambda e, f, idx: (layer, e, 0, f)),
                  pl.BlockSpec((None, None, d, tf), lambda e, f, idx: (layer, e, 0, f)),
                  pl.BlockSpec((None, None, tf, d), lambda e, f, idx: (layer, e, f, 0)),
                  pl.BlockSpec((None, cap, 1), lambda e, f, idx: (e, 0, 0))],
        out_specs=pl.BlockSpec((None, cap, d), lambda e, f, idx: (e, 0, 0),
                               pipeline_mode=pl.Buffered(1)),
        scratch_shapes=[pltpu.VMEM((nf * _gather_rows_per_step(cap, nf), d), F32),
                        pltpu.VMEM((cap, d), BF16), pltpu.SemaphoreType.DMA(())],
    )
    return pl.pallas_call(
        functools.partial(_experts_kernel, rc=_row_chunk(cap), cap_lat=cap_lat, nf=nf),
        grid_spec=grid_spec,
        out_shape=jax.ShapeDtypeStruct((ne, cap, d), F32),
        compiler_params=_cparams("arbitrary", "arbitrary"),
        name="experts",
    )(idx, xs, mods, w_gate, w_up, w_down, gsel)


def _route(aff, n_tok):
    cap = max(1, EC_CAPACITY_FACTOR * n_tok // N_EXPERTS)
    g, idx = lax.top_k(aff.T, cap)
    return g, idx


COMBINE_CHUNK = 256


def _combine_kernel(*refs, alpha, n_chunks, with_next):
    if with_next:
        (src_ref, bnd_ref, ye_hbm, tok_ref, x_ref, mod_ref, g_ref, b_ref, modn_ref, xo_ref,
         u_ref, buf_ref, acc_ref, cnt_ref, sem) = refs
    else:
        (src_ref, bnd_ref, ye_hbm, tok_ref, x_ref, mod_ref, g_ref, b_ref, xo_ref,
         buf_ref, acc_ref, cnt_ref, sem) = refs
    b = pl.program_id(0)
    tb = x_ref.shape[0]
    cs = COMBINE_CHUNK

    def issue(kk):
        slot = kk % 2
        base = kk * cs

        def one(i, carry):
            pltpu.make_async_copy(ye_hbm.at[pl.ds(src_ref[base + i], 1), :],
                                  buf_ref.at[slot, pl.ds(i, 1), :], sem.at[slot]).start()
            return carry
        lax.fori_loop(0, cs, one, 0, unroll=8)
        cnt_ref[0] = kk + 1

    @pl.when(b == 0)
    def _():
        cnt_ref[0] = 0
        cnt_ref[1] = 0
        issue(0)

    lo, hi = bnd_ref[b], bnd_ref[b + 1]
    k0 = lo // cs
    k1 = jnp.where(hi > lo, (hi + cs - 1) // cs, k0)
    acc_ref[...] = jnp.zeros_like(acc_ref)
    row_tok = b * tb + lax.broadcasted_iota(jnp.int32, (tb, cs), 0)

    def chunk(kk, carry):
        slot = kk % 2

        @pl.when(kk >= cnt_ref[1])
        def _():
            pltpu.make_async_copy(buf_ref.at[slot], buf_ref.at[slot], sem.at[slot]).wait()
            cnt_ref[1] = kk + 1

        @pl.when(jnp.logical_and(kk + 1 < n_chunks, kk + 1 >= cnt_ref[0]))
        def _():
            issue(kk + 1)

        toks = tok_ref[:, pl.ds(pl.multiple_of(kk * cs, cs), cs)]
        onehot = jnp.where(row_tok == toks, 1.0, 0.0).astype(BF16)
        acc_ref[...] += jnp.dot(onehot, buf_ref[slot].astype(BF16), preferred_element_type=F32)
        return carry

    lax.fori_loop(k0, k1, chunk, 0)

    mod = mod_ref[...]
    xn = _resid_ln(x_ref[...], acc_ref[...], mod[5:6, :], g_ref[...], b_ref[...], alpha)
    xo_ref[...] = xn
    if with_next:
        modn = modn_ref[...]
        u_ref[...] = (xn * (1.0 + modn[1:2, :]) + modn[0:1, :]).astype(u_ref.dtype)


def _combine_call(ye, idx, xs1, mods, ln_g, ln_b, rows, n_lat_tiles, alpha, mods_next=None):
    ne, cap, d = ye.shape
    tb = ROW_TILE
    cs = COMBINE_CHUNK
    nb = rows // tb
    n_list = ne * cap
    n_pad = -(-n_list // cs) * cs
    tok_flat = idx.reshape(-1)
    order = jnp.argsort(tok_flat).astype(jnp.int32)
    tok_sorted = jnp.take(tok_flat, order)
    bounds = jnp.searchsorted(tok_sorted, jnp.arange(nb + 1, dtype=jnp.int32) * tb).astype(jnp.int32)
    tok_pad = jnp.full((1, n_pad), -1, jnp.int32).at[0, :n_list].set(tok_sorted)
    src_pad = jnp.zeros((n_pad,), jnp.int32).at[:n_list].set(order)
    row = pl.BlockSpec((tb, d), lambda i, *_: (i, 0))
    vec = pl.BlockSpec((1, d), lambda i, *_: (0, 0))
    mod_spec = pl.BlockSpec((None, N_MOD, d),
                            lambda i, *_: (jnp.where(i >= n_lat_tiles, 1, 0), 0, 0))
    with_next = mods_next is not None
    in_specs = [pl.BlockSpec(memory_space=pl.ANY), pl.BlockSpec((1, n_pad), lambda i, *_: (0, 0)),
                row, mod_spec, vec, vec]
    args = [src_pad, bounds, ye.reshape(n_list, d), tok_pad, xs1, mods,
            ln_g.reshape(1, d), ln_b.reshape(1, d)]
    out_specs, out_shape = row, jax.ShapeDtypeStruct((rows, d), F32)
    if with_next:
        in_specs.append(mod_spec)
        args.append(mods_next)
        out_specs = [row, row]
        out_shape = [out_shape, jax.ShapeDtypeStruct((rows, d), BF16)]
    grid_spec = pltpu.PrefetchScalarGridSpec(
        num_scalar_prefetch=2,
        grid=(nb,),
        in_specs=in_specs,
        out_specs=out_specs,
        scratch_shapes=[pltpu.VMEM((2, cs, d), F32), pltpu.VMEM((tb, d), F32),
                        pltpu.SMEM((2,), jnp.int32), pltpu.SemaphoreType.DMA((2,))],
    )
    return pl.pallas_call(
        functools.partial(_combine_kernel, alpha=alpha, n_chunks=n_pad // cs,
                          with_next=with_next),
        grid_spec=grid_spec,
        out_shape=out_shape,
        compiler_params=_cparams("arbitrary"),
        name="moe_combine_ln",
    )(*args)


def _spread_matrix(n_heads, direction):
    r = jnp.arange(4 * n_heads)[:, None] % (2 * n_heads)
    ch = jnp.arange(n_heads * HEAD_DIM)[None, :] // HEAD_DIM
    return (r == ch + direction * n_heads).astype(BF16)


def kernel(x, c, ctx, c_ctx, w_ada, b_ada, w_in, ssm_conv_w, ssm_conv_b, ssm_dt_bias, ssm_a_log,
           ssm_d, ssm_norm_w, w_ssm_out, conv_dw_w, conv_dw_b, conv_ln_g, conv_ln_b, w_conv_out,
           w_o, ln1_g, ln1_b, w_router, w_exp_gate, w_exp_up, w_exp_down, ln2_g, ln2_b):
    batch, n, d = x.shape
    nc = ctx.shape[1]
    depth = w_ada.shape[0]
    assert batch == 1 and n % ROW_TILE == 0 and nc % ROW_TILE == 0 and n % nc == 0
    d_inner = 2 * d
    n_heads = d_inner // HEAD_DIM
    d_bc = GROUPS * STATE
    d_xbc = d_inner + 2 * d_bc
    d_conv = d
    o_xbc = d_inner
    o_dt = o_xbc + d_xbc
    o_glu = o_dt + 2 * n_heads
    alpha = (2 * depth) ** 0.25
    nl, ncc = n // CHUNK, nc // CHUNK
    n_lat_tiles = n // ROW_TILE

    xs = (x[0], ctx[0])
    u_in = None
    cvec = jnp.zeros((SUBLANES, d), F32).at[0].set(c[0]).at[1].set(c_ctx)
    mods_all = _ada_call(cvec, w_ada, b_ada)
    e_mats = [_spread_matrix(n_heads, dr) for dr in range(2)]

    for i in range(depth):
        last = i == depth - 1
        mods = mods_all[i, :2].reshape(2, N_MOD, d)
        rows_all = n + nc
        rows = n if last else rows_all

        if u_in is None:
            u_in = _modulate_call(xs, mods, rows_all, n_lat_tiles)
        z = _mm_call(u_in, w_in, i, 0, d_inner, rows=rows, name="proj_z")
        xbc = _mm_call(u_in, w_in, i, o_xbc, d_xbc, rows=rows_all, name="proj_xbc",
                       slab_major=True)
        dtraw = _mm_call(u_in, w_in, i, o_dt, 2 * n_heads, rows=rows_all, name="proj_dt")
        glu = _mm_call(u_in, w_in, i, o_glu, 2 * d_conv + 2 * d, rows=rows, name="proj_glu_gate")

        conv_w = jnp.zeros((SUBLANES, d_xbc), F32).at[:SSM_CONV].set(ssm_conv_w[i])
        conv_b = ssm_conv_b[i].reshape(1, d_xbc)
        dt_bias = ssm_dt_bias[i].reshape(1, 2 * n_heads)
        a_row = -jnp.exp(ssm_a_log[i].astype(F32)).reshape(1, 2 * n_heads)
        d_row = jnp.repeat(ssm_d[i], HEAD_DIM).reshape(1, d_inner)
        y = _ssd_call(xbc, dtraw, conv_w, conv_b, dt_bias, a_row, e_mats, d_row,
                      nl=nl, ncc=ncc, d_inner=d_inner)
        a_act = _gnorm_call(y, z, ssm_norm_w[i].reshape(1, d_inner), rows, d_inner)

        hcv = _glu_conv_call(glu, conv_dw_w[i], conv_dw_b[i], n=n, nc=rows - n, d_conv=d_conv)
        b_act = _lnswish_call([hcv], conv_ln_g[i], conv_ln_b[i])

        t1 = _mm_call(a_act, w_ssm_out, i, 0, d, rows=rows, name="ssm_out",
                      epi_fn=lambda acc, gt: _sigmoid(gt) * acc,
                      epi_cols=[2 * d_conv], epi_args=[glu])
        t2 = _mm_call(b_act, w_conv_out, i, 0, d, rows=rows, name="conv_out", out_dtype=BF16,
                      epi_fn=lambda acc, gt, prev: prev + _sigmoid(gt) * acc,
                      epi_cols=[2 * d_conv + d, 0], epi_args=[glu, t1])
        mix = _mm_call(t2, w_o, i, 0, d, rows=rows, name="w_o")

        xs1, aff = _resid_call(xs, mix, mods, ln1_g[i], ln1_b[i], rows, n_lat_tiles, alpha,
                               w_router=w_router[i])

        g_l, idx_l = _route(aff[:n], n)
        gsel, idx = g_l, idx_l
        if not last:
            g_c, idx_c = _route(aff[n:], nc)
            gsel = jnp.concatenate([g_l, g_c], axis=1)
            idx = jnp.concatenate([idx_l, idx_c + n], axis=1)
        ye = _experts_call(idx, xs1, mods, w_exp_gate, w_exp_up, w_exp_down, gsel[..., None], i,
                           idx_l.shape[1])
        if last:
            xs = _combine_call(ye, idx, xs1, mods, ln2_g[i], ln2_b[i], rows, n_lat_tiles, alpha)
        else:
            mods_next = mods_all[i + 1, :2].reshape(2, N_MOD, d)
            xs, u_in = _combine_call(ye, idx, xs1, mods, ln2_g[i], ln2_b[i], rows, n_lat_tiles,
                                     alpha, mods_next=mods_next)

    return xs[:n].reshape(1, n, d)
```
